```python
import math
import jax, jax.numpy as jnp
from jax import lax
import numpy as np

D_MODEL = 1024
BATCH = 8
SEQ = 4096
DEPTH = 2

CTX_LEN = 256
GRID_W = 64
EPS = 1e-6
ROPE_BASE = 10000.0
D_FF = 4 * D_MODEL
GDN_HEADS = 4
GDN_DK = 128
GDN_DV = 128
GDN_CHUNK = 64
CONV_W = 5
MLA_HEADS = 4
MLA_Q_RANK = 256
MLA_KV_RANK = 128
MLA_NOPE = 64
MLA_ROPE = 32
MLA_DV = 64
Q_BLOCK = 128
HG_HEADS = 4
HG_DK = 128
HG_DV = 64
HG_CHUNK = 32
D_MIX = GDN_HEADS * GDN_DV + MLA_HEADS * MLA_DV + HG_HEADS * HG_DV
IN_SIZES = (GDN_HEADS * GDN_DK, GDN_HEADS * GDN_DK, GDN_HEADS * GDN_DV, GDN_HEADS * GDN_DV, 2 * GDN_HEADS, 2 * GDN_HEADS,
            MLA_Q_RANK, MLA_KV_RANK, MLA_ROPE,
            HG_HEADS * HG_DK, HG_HEADS * HG_DK, HG_HEADS * HG_DK, HG_HEADS * HG_DV, HG_HEADS * HG_DV)
D_IN = sum(IN_SIZES)

kernel_name = "hybrid_gdn_mla_hgrn2_dit_block"

F32 = jnp.float32


def split_cols(z, sizes):
    return jnp.split(z, np.cumsum(np.array(sizes))[:-1].tolist(), axis=-1)


def rms_norm(x, g):
    xf = x.astype(F32)
    y = xf * lax.rsqrt(jnp.mean(xf * xf, axis=-1, keepdims=True) + EPS)
    return (y * g.astype(F32)).astype(x.dtype)


def l2_norm(x):
    xf = x.astype(F32)
    return xf * lax.rsqrt(jnp.sum(xf * xf, axis=-1, keepdims=True) + EPS)


def modulate(h, shift, scale):
    return h * (1.0 + scale) + shift


def short_conv(u, w):
    y = lax.conv_general_dilated(u, w[:, None, :].astype(u.dtype), window_strides=(1,),
                                 padding=[(CONV_W // 2, CONV_W // 2)],
                                 dimension_numbers=("NWC", "WIO", "NWC"),
                                 feature_group_count=u.shape[-1])
    return jax.nn.silu(y)


def axial_rope(rows, dim):
    per_axis = dim // 2
    inv = ROPE_BASE ** (-jnp.arange(0, per_axis, 2, dtype=F32) / per_axis)
    row = jnp.repeat(jnp.arange(rows, dtype=F32), GRID_W)
    col = jnp.tile(jnp.arange(GRID_W, dtype=F32), rows)
    ang = jnp.concatenate([row[:, None] * inv, col[:, None] * inv], axis=-1)
    return jnp.cos(ang), jnp.sin(ang)


def apply_rope(t, cos, sin):
    half = t.shape[-1] // 2
    t1 = t[..., :half].astype(F32)
    t2 = t[..., half:].astype(F32)
    return jnp.concatenate([t1 * cos - t2 * sin, t1 * sin + t2 * cos], axis=-1).astype(t.dtype)


def to_chunks(t, chunk):
    b, l, h = t.shape[:3]
    t = t.reshape((b, l // chunk, chunk, h) + t.shape[3:])
    return jnp.swapaxes(jnp.moveaxis(t, 1, 0), 2, 3)


def from_chunks(t):
    n, b, h, c = t.shape[:4]
    t = jnp.moveaxis(jnp.swapaxes(t, 2, 3), 0, 1)
    return t.reshape((b, n * c, h) + t.shape[4:])


def gdn_scan(q, k, v, log_a, beta, s0, want_out):
    kc = to_chunks(k.astype(F32), GDN_CHUNK)
    vc = to_chunks(v.astype(F32), GDN_CHUNK)
    bc = to_chunks(beta.astype(F32), GDN_CHUNK)
    gcum = jnp.cumsum(to_chunks(log_a.astype(F32), GDN_CHUNK), axis=-1)
    idx = jnp.arange(GDN_CHUNK)
    incl = idx[:, None] >= idx[None, :]
    strict = idx[:, None] > idx[None, :]
    decay = jnp.exp(jnp.where(incl, gcum[..., :, None] - gcum[..., None, :], -jnp.inf))
    kb = kc * bc[..., None]
    lower = jnp.where(strict, jnp.einsum("nbhid,nbhjd->nbhij", kb, kc) * decay, 0.0)
    eye = jnp.eye(GDN_CHUNK, dtype=F32)
    t_inv = lax.linalg.triangular_solve(eye + lower, jnp.broadcast_to(eye, lower.shape),
                                        left_side=True, lower=True, unit_diagonal=True)
    u = jnp.einsum("nbhij,nbhjd->nbhid", t_inv, vc * bc[..., None])
    w = jnp.einsum("nbhij,nbhjd->nbhid", t_inv, kb * jnp.exp(gcum)[..., None])
    g_last = gcum[..., -1]
    k_end = kc * jnp.exp(g_last[..., None] - gcum)[..., None]
    if want_out:
        qc = to_chunks(q.astype(F32), GDN_CHUNK)
        q_att = jnp.einsum("nbhid,nbhjd->nbhij", qc, kc) * decay
        q_dec = qc * jnp.exp(gcum)[..., None]
        xs = (u, w, k_end, g_last, q_att, q_dec)
    else:
        xs = (u, w, k_end, g_last)

    def step(s, inp):
        v_new = inp[0] - jnp.einsum("bhcd,bhde->bhce", inp[1], s)
        s_next = s * jnp.exp(inp[3])[..., None, None] + jnp.einsum("bhcd,bhce->bhde", inp[2], v_new)
        if want_out:
            o = jnp.einsum("bhcd,bhde->bhce", inp[5], s) + jnp.einsum("bhij,bhje->bhie", inp[4], v_new)
            return s_next, o
        return s_next, None

    s_fin, o = lax.scan(step, s0, xs)
    return (from_chunks(o) if want_out else None), s_fin


def gdn_mixer(pc, px, conv_w, a_log, dt_bias, norm_g, want_ctx):
    def prep(q, k, v, b_raw, a_raw):
        b, l = q.shape[:2]
        qkv = short_conv(jnp.concatenate([q, k, v], axis=-1), conv_w)
        q, k, v = split_cols(qkv, (GDN_HEADS * GDN_DK, GDN_HEADS * GDN_DK, GDN_HEADS * GDN_DV))
        q = l2_norm(q.reshape(b, l, GDN_HEADS, GDN_DK)) * GDN_DK ** -0.5
        k = l2_norm(k.reshape(b, l, GDN_HEADS, GDN_DK))
        v = v.reshape(b, l, GDN_HEADS, GDN_DV)
        beta = jax.nn.sigmoid(b_raw.astype(F32)).reshape(b, l, 2, GDN_HEADS)
        log_a = -jnp.exp(a_log.astype(F32)) * jax.nn.softplus(
            a_raw.astype(F32).reshape(b, l, 2, GDN_HEADS) + dt_bias.astype(F32))
        return q, k, v, beta, log_a

    cq, ck, cv, cbeta, cla = prep(pc[0], pc[1], pc[2], pc[4], pc[5])
    xq, xk, xv, xbeta, xla = prep(px[0], px[1], px[2], px[4], px[5])
    bsz = xq.shape[0]
    outs_x, outs_c = [], []
    for d in range(2):
        fl = (lambda t: jnp.flip(t, axis=1)) if d == 1 else (lambda t: t)
        s0 = jnp.zeros((bsz, GDN_HEADS, GDN_DK, GDN_DV), F32)
        oc, s_ctx = gdn_scan(fl(cq), fl(ck), fl(cv), fl(cla[:, :, d]), fl(cbeta[:, :, d]), s0, want_ctx)
        ox, _ = gdn_scan(fl(xq), fl(xk), fl(xv), fl(xla[:, :, d]), fl(xbeta[:, :, d]), s_ctx, True)
        outs_x.append(fl(ox))
        if want_ctx:
            outs_c.append(fl(oc))

    def readout(o, gate):
        b, l = gate.shape[:2]
        y = rms_norm(o, norm_g) * jax.nn.silu(gate.astype(F32).reshape(b, l, GDN_HEADS, GDN_DV))
        return y.reshape(b, l, GDN_HEADS * GDN_DV).astype(gate.dtype)

    y_x = readout(outs_x[0] + outs_x[1], px[3])
    y_c = readout(outs_c[0] + outs_c[1], pc[3]) if want_ctx else None
    return y_x, y_c


def mla_kv(ckv, kr, kv_norm_g, w_ukv, cos, sin):
    b, l = ckv.shape[:2]
    kv = (rms_norm(ckv, kv_norm_g) @ w_ukv).reshape(b, l, MLA_HEADS, MLA_NOPE + MLA_DV)
    if cos is not None:
        kr = apply_rope(kr, cos, sin)
    k_rope = jnp.broadcast_to(kr[:, :, None, :], (b, l, MLA_HEADS, MLA_ROPE))
    return jnp.concatenate([kv[..., :MLA_NOPE], k_rope], axis=-1), kv[..., MLA_NOPE:]


def mla_q(cq, q_norm_g, w_uq, cos, sin):
    b, l = cq.shape[:2]
    q = (rms_norm(cq, q_norm_g) @ w_uq).reshape(b, l, MLA_HEADS, MLA_NOPE + MLA_ROPE)
    if cos is not None:
        q = jnp.concatenate([q[..., :MLA_NOPE], apply_rope(q[..., MLA_NOPE:], cos[:, None], sin[:, None])], axis=-1)
    return q


def block_attention(q, k, v):
    b, lq, h, dq = q.shape
    qb = jnp.moveaxis(q.reshape(b, lq // Q_BLOCK, Q_BLOCK, h, dq), 1, 0)
    scale = dq ** -0.5

    def one(qblk):
        s = jnp.einsum("bqhd,bkhd->bhqk", qblk, k).astype(F32) * scale
        p = jax.nn.softmax(s, axis=-1).astype(v.dtype)
        return jnp.einsum("bhqk,bkhd->bqhd", p, v)

    o = lax.map(one, qb)
    return jnp.moveaxis(o, 0, 1).reshape(b, lq, h * v.shape[-1])


def mla_mixer(pc, px, q_norm_g, kv_norm_g, w_uq, w_ukv, cos, sin, want_ctx):
    kc, vc = mla_kv(pc[1], pc[2], kv_norm_g, w_ukv, None, None)
    kx, vx = mla_kv(px[1], px[2], kv_norm_g, w_ukv, cos, sin)
    qx = mla_q(px[0], q_norm_g, w_uq, cos, sin)
    y_x = block_attention(qx, jnp.concatenate([kc, kx], axis=1), jnp.concatenate([vc, vx], axis=1))
    y_c = block_attention(mla_q(pc[0], q_norm_g, w_uq, None, None), kc, vc) if want_ctx else None
    return y_x, y_c


def hgrn_scan(q, k, v, log_f, s0, want_out):
    if want_out:
        xs = (to_chunks(k, HG_CHUNK), to_chunks(v, HG_CHUNK), to_chunks(log_f, HG_CHUNK), to_chunks(q, HG_CHUNK))
    else:
        xs = (to_chunks(k, HG_CHUNK), to_chunks(v, HG_CHUNK), to_chunks(log_f, HG_CHUNK))
    idx = jnp.arange(HG_CHUNK)
    incl = idx[:, None] >= idx[None, :]

    def step(s, inp):
        k_n, v_n, lf_n = inp[0], inp[1], inp[2]
        bcum = jnp.cumsum(lf_n, axis=2)
        b_last = bcum[:, :, -1]
        s_next = s * jnp.exp(b_last)[..., None] + jnp.einsum(
            "bhcd,bhce->bhde", k_n * jnp.exp(b_last[:, :, None] - bcum), v_n)
        if want_out:
            q_n = inp[3]
            dec = jnp.exp(jnp.where(incl[:, :, None], bcum[:, :, :, None, :] - bcum[:, :, None, :, :], -jnp.inf))
            att = jnp.einsum("bhid,bhjd,bhijd->bhij", q_n, k_n, dec)
            o = jnp.einsum("bhcd,bhde->bhce", q_n * jnp.exp(bcum), s) + jnp.einsum("bhij,bhje->bhie", att, v_n)
            return s_next, o
        return s_next, None

    s_fin, o = lax.scan(step, s0, xs)
    return (from_chunks(o) if want_out else None), s_fin


def hgrn_mixer(pc, px, lb, norm_g, want_ctx):
    lbf = lb.astype(F32).reshape(2, HG_HEADS, HG_DK)

    def prep(q, f_fwd, f_bwd, i):
        b, l = q.shape[:2]
        q = q.astype(F32).reshape(b, l, HG_HEADS, HG_DK) * HG_DK ** -0.5
        i = i.astype(F32).reshape(b, l, HG_HEADS, HG_DV)
        log_fs = [jnp.logaddexp(jnp.log(lbf[d]), jnp.log1p(-lbf[d]) +
                                jax.nn.log_sigmoid(fr.astype(F32).reshape(b, l, HG_HEADS, HG_DK)))
                  for d, fr in enumerate((f_fwd, f_bwd))]
        return q, i, log_fs

    cq, ci, clf = prep(pc[0], pc[1], pc[2], pc[3])
    xq, xi, xlf = prep(px[0], px[1], px[2], px[3])
    bsz = xq.shape[0]
    outs_x, outs_c = [], []
    for d in range(2):
        fl = (lambda t: jnp.flip(t, axis=1)) if d == 1 else (lambda t: t)
        s0 = jnp.zeros((bsz, HG_HEADS, HG_DK, HG_DV), F32)
        oc, s_ctx = hgrn_scan(fl(cq), fl(-jnp.expm1(clf[d])), fl(ci), fl(clf[d]), s0, want_ctx)
        ox, _ = hgrn_scan(fl(xq), fl(-jnp.expm1(xlf[d])), fl(xi), fl(xlf[d]), s_ctx, True)
        outs_x.append(fl(ox))
        if want_ctx:
            outs_c.append(fl(oc))

    def readout(o, gate):
        b, l = gate.shape[:2]
        y = rms_norm(o, norm_g) * jax.nn.silu(gate.astype(F32).reshape(b, l, HG_HEADS, HG_DV))
        return y.reshape(b, l, HG_HEADS * HG_DV).astype(gate.dtype)

    y_x = readout(outs_x[0] + outs_x[1], px[4])
    y_c = readout(outs_c[0] + outs_c[1], pc[4]) if want_ctx else None
    return y_x, y_c


def mixing_layer(hx, hc, w_in_l, w_out_l, gdn_conv_l, gdn_a_log_l, gdn_dt_bias_l, gdn_norm_l,
                 mla_q_norm_l, mla_kv_norm_l, mla_w_uq_l, mla_w_ukv_l, lb_l, hgrn_norm_l, cos, sin, want_ctx):
    zx = split_cols(hx @ w_in_l, IN_SIZES)
    zc = split_cols(hc @ w_in_l, IN_SIZES)
    gdn_x, gdn_c = gdn_mixer(zc[0:6], zx[0:6], gdn_conv_l, gdn_a_log_l, gdn_dt_bias_l, gdn_norm_l, want_ctx)
    mla_x, mla_c = mla_mixer(zc[6:9], zx[6:9], mla_q_norm_l, mla_kv_norm_l, mla_w_uq_l, mla_w_ukv_l, cos, sin, want_ctx)
    hg_x, hg_c = hgrn_mixer(zc[9:14], zx[9:14], lb_l, hgrn_norm_l, want_ctx)
    y_x = jnp.concatenate([gdn_x, mla_x, hg_x], axis=-1) @ w_out_l
    y_c = jnp.concatenate([gdn_c, mla_c, hg_c], axis=-1) @ w_out_l if want_ctx else None
    return y_x, y_c


def sq_relu_mlp(h, w1, w2):
    return jnp.square(jax.nn.relu(h @ w1)) @ w2


def setup_inputs(seed: int = 0) -> dict:
    key = jax.random.key(seed)
    ks = jax.random.split(key, 24)

    def nrm(k, shape, scale):
        return jax.random.normal(k, shape, F32) * scale

    def gain(k, shape):
        return 1.0 + 0.02 * jax.random.normal(k, shape, F32)

    dt = jnp.exp(jax.random.uniform(ks[13], (DEPTH, 2, GDN_HEADS), F32, math.log(1e-3), math.log(1e-1)))
    return {
        "x": nrm(ks[0], (BATCH, SEQ, D_MODEL), 1.0),
        "c": nrm(ks[1], (BATCH, D_MODEL), 1.0),
        "ctx": nrm(ks[2], (BATCH, CTX_LEN, D_MODEL), 1.0),
        "c_ctx": nrm(ks[3], (D_MODEL,), 1.0),
        "w_ada": nrm(ks[4], (DEPTH, D_MODEL, 6 * D_MODEL), 0.5 * D_MODEL ** -0.5),
        "b_ada": nrm(ks[5], (DEPTH, 6 * D_MODEL), 0.02),
        "norm_mix_pre": gain(ks[6], (DEPTH, D_MODEL)),
        "norm_mix_post": gain(ks[7], (DEPTH, D_MODEL)),
        "norm_mlp_pre": gain(ks[8], (DEPTH, D_MODEL)),
        "norm_mlp_post": gain(ks[9], (DEPTH, D_MODEL)),
        "w_in": nrm(ks[10], (DEPTH, D_MODEL, D_IN), D_MODEL ** -0.5),
        "w_out": nrm(ks[11], (DEPTH, D_MIX, D_MODEL), D_MIX ** -0.5),
        "gdn_conv": nrm(ks[12], (DEPTH, CONV_W, GDN_HEADS * (2 * GDN_DK + GDN_DV)), CONV_W ** -0.5),
        "gdn_a_log": jnp.log(jax.random.uniform(ks[14], (DEPTH, 2, GDN_HEADS), F32, 1.0, 16.0)),
        "gdn_dt_bias": dt + jnp.log(-jnp.expm1(-dt)),
        "gdn_norm": gain(ks[15], (DEPTH, GDN_DV)),
        "mla_q_norm": gain(ks[16], (DEPTH, MLA_Q_RANK)),
        "mla_kv_norm": gain(ks[17], (DEPTH, MLA_KV_RANK)),
        "mla_w_uq": nrm(ks[18], (DEPTH, MLA_Q_RANK, MLA_HEADS * (MLA_NOPE + MLA_ROPE)), MLA_Q_RANK ** -0.5),
        "mla_w_ukv": nrm(ks[19], (DEPTH, MLA_KV_RANK, MLA_HEADS * (MLA_NOPE + MLA_DV)), MLA_KV_RANK ** -0.5),
        "hgrn_lb_logits": nrm(ks[20], (DEPTH, 2, HG_HEADS * HG_DK), 0.1),
        "hgrn_norm": gain(ks[21], (DEPTH, HG_DV)),
        "w_mlp1": nrm(ks[22], (DEPTH, D_MODEL, D_FF), D_MODEL ** -0.5),
        "w_mlp2": nrm(ks[23], (DEPTH, D_FF, D_MODEL), D_FF ** -0.5),
    }


def reference(x, c, ctx, c_ctx, w_ada, b_ada, norm_mix_pre, norm_mix_post, norm_mlp_pre, norm_mlp_post,
              w_in, w_out, gdn_conv, gdn_a_log, gdn_dt_bias, gdn_norm, mla_q_norm, mla_kv_norm,
              mla_w_uq, mla_w_ukv, hgrn_lb_logits, hgrn_norm, w_mlp1, w_mlp2):
    rows = x.shape[1] // GRID_W
    cos, sin = axial_rope(rows, MLA_ROPE)
    lb_cum = jnp.cumsum(jax.nn.softmax(hgrn_lb_logits.astype(F32), axis=0), axis=0)
    lower_bounds = lb_cum - lb_cum[0]
    h_ctx = ctx
    for l in range(DEPTH):
        want_ctx = l < DEPTH - 1
        sh1x, sc1x, g1x, sh2x, sc2x, g2x = [m[:, None, :] for m in
                                            split_cols(jax.nn.silu(c) @ w_ada[l] + b_ada[l], (D_MODEL,) * 6)]
        sh1c, sc1c, g1c, sh2c, sc2c, g2c = split_cols(jax.nn.silu(c_ctx) @ w_ada[l] + b_ada[l], (D_MODEL,) * 6)
        hx = modulate(rms_norm(x, norm_mix_pre[l]), sh1x, sc1x)
        hc = modulate(rms_norm(h_ctx, norm_mix_pre[l]), sh1c, sc1c)
        y_x, y_c = mixing_layer(hx, hc, w_in[l], w_out[l], gdn_conv[l], gdn_a_log[l], gdn_dt_bias[l], gdn_norm[l],
                                mla_q_norm[l], mla_kv_norm[l], mla_w_uq[l], mla_w_ukv[l], lower_bounds[l],
                                hgrn_norm[l], cos, sin, want_ctx)
        x = x + g1x * rms_norm(y_x, norm_mix_post[l])
        hx = modulate(rms_norm(x, norm_mlp_pre[l]), sh2x, sc2x)
        x = x + g2x * rms_norm(sq_relu_mlp(hx, w_mlp1[l], w_mlp2[l]), norm_mlp_post[l])
        if want_ctx:
            h_ctx = h_ctx + g1c * rms_norm(y_c, norm_mix_post[l])
            hc = modulate(rms_norm(h_ctx, norm_mlp_pre[l]), sh2c, sc2c)
            h_ctx = h_ctx + g2c * rms_norm(sq_relu_mlp(hc, w_mlp1[l], w_mlp2[l]), norm_mlp_post[l])
    return x
```

```python
import functools
import math

import jax
import jax.numpy as jnp
from jax import lax
from jax.experimental import pallas as pl
from jax.experimental.pallas import tpu as pltpu

F32 = jnp.float32
BF16 = jnp.bfloat16

D_MODEL = 1024
D_FF = 4 * D_MODEL
GRID_W = 64
EPS = 1e-6
ROPE_BASE = 10000.0
GDN_HEADS, GDN_DK, GDN_DV, CONV_W = 4, 128, 128, 5
MLA_HEADS, MLA_Q_RANK, MLA_KV_RANK, MLA_NOPE, MLA_ROPE, MLA_DV = 4, 256, 128, 64, 32, 64
HG_HEADS, HG_DK, HG_DV = 4, 128, 64
IN_SIZES = (512, 512, 512, 512, 8, 8, 256, 128, 32, 512, 512, 512, 256, 256)

LANE = 128
VMEM_LIMIT = 56 * 1024 * 1024

Z_QKV, Z_GG, Z_HQ, Z_HFF, Z_HFB, Z_HI, Z_HGATE, Z_CQ, Z_BA, Z_CKV, Z_KR, Z_KRP, NZ = (
    0, 1536, 2048, 2560, 3072, 3584, 4096, 4352, 4608, 4736, 4864, 4992, 5120)

SCAN_BLOCK = 256
GDN_C = 64
GDN_SUB = 16
HG_C = 64


def _cp(sem, vmem=VMEM_LIMIT):
    return pltpu.CompilerParams(dimension_semantics=sem, vmem_limit_bytes=vmem)


def _dot(a, b):
    return jnp.dot(a, b, preferred_element_type=F32)


def _dot_nt(a, b):
    return lax.dot_general(a, b, (((1,), (1,)), ((), ())), preferred_element_type=F32)


def _dot_tn(a, b):
    return lax.dot_general(a, b, (((0,), (0,)), ((), ())), preferred_element_type=F32)


def _split3(x):
    hi = x.astype(BF16)
    r = x - hi.astype(F32)
    mid = r.astype(BF16)
    lo = (r - mid.astype(F32)).astype(BF16)
    return hi, mid, lo


def _dot01(m01, x):
    hi, mid, lo = _split3(x)
    return _dot(m01, hi) + _dot(m01, mid) + _dot(m01, lo)


def _dot01_tn(x, m01):
    hi, mid, lo = _split3(x)
    return _dot_tn(hi, m01) + _dot_tn(mid, m01) + _dot_tn(lo, m01)


def _sigmoid(x):
    return 1.0 / (1.0 + jnp.exp(-x))


def _softplus(x):
    return jnp.maximum(x, 0.0) + jnp.log(1.0 + jnp.exp(-jnp.abs(x)))


def _rms(x, g):
    ms = jnp.mean(x * x, axis=-1, keepdims=True)
    return x * lax.rsqrt(ms + EPS) * g


def _ada_kernel(c_ref, w_ref, b_ref, o_ref):
    c = c_ref[...]
    s = c * _sigmoid(c)
    o_ref[...] = jnp.dot(s, w_ref[...], preferred_element_type=F32,
                         precision=lax.Precision.HIGHEST) + b_ref[...]


def _ada(cvec, w, b):
    n = w.shape[1]
    tn = 1024
    return pl.pallas_call(
        _ada_kernel, name="ada_mod",
        grid=(n // tn,),
        in_specs=[pl.BlockSpec(cvec.shape, lambda j: (0, 0)),
                  pl.BlockSpec((w.shape[0], tn), lambda j: (0, j)),
                  pl.BlockSpec((1, tn), lambda j: (0, j))],
        out_specs=pl.BlockSpec((cvec.shape[0], tn), lambda j: (0, j)),
        out_shape=jax.ShapeDtypeStruct((cvec.shape[0], n), F32),
        compiler_params=_cp(("parallel",)),
    )(cvec, w, b.reshape(1, n))


def _in_proj_kernel(h_ref, mod_ref, g_ref, w_ref, z_ref, hm_ref):
    @pl.when(pl.program_id(1) == 0)
    def _():
        m = mod_ref[0]
        y = _rms(h_ref[...], g_ref[...])
        hm_ref[...] = (y * (1.0 + m[:, D_MODEL:2 * D_MODEL]) + m[:, 0:D_MODEL]).astype(BF16)

    z_ref[...] = _dot(hm_ref[...], w_ref[...]).astype(BF16)


def _mod_index(i, tm, n_lat_rows, seq):
    lat_tiles = n_lat_rows // tm
    return jnp.where(i < lat_tiles, 1 + i // (seq // tm), 0)


def _in_proj(h, mods, g, w, n_lat_rows, seq, tm=1024, tn=512):
    t = h.shape[0]
    return pl.pallas_call(
        _in_proj_kernel, name="in_proj",
        grid=(t // tm, NZ // tn),
        in_specs=[pl.BlockSpec((tm, D_MODEL), lambda i, j: (i, 0)),
                  pl.BlockSpec((1, 1, 6 * D_MODEL), lambda i, j: (_mod_index(i, tm, n_lat_rows, seq), 0, 0)),
                  pl.BlockSpec((1, D_MODEL), lambda i, j: (0, 0)),
                  pl.BlockSpec((D_MODEL, tn), lambda i, j: (0, j))],
        out_specs=pl.BlockSpec((tm, tn), lambda i, j: (i, j)),
        out_shape=jax.ShapeDtypeStruct((t, NZ), BF16),
        scratch_shapes=[pltpu.VMEM((tm, D_MODEL), BF16)],
        compiler_params=_cp(("parallel", "arbitrary")),
    )(h, mods, g.reshape(1, D_MODEL), w)


HALO = 16


def _gdn_prep_kernel(zc_ref, zp_ref, zn_ref, cw_ref, o_ref, buf_ref, *, tm, lat_tiles, seq_tiles, ctx_tiles):
    i = pl.program_id(0)
    r = jnp.where(i < lat_tiles, i % seq_tiles, (i - lat_tiles) % ctx_tiles)
    n = jnp.where(i < lat_tiles, seq_tiles, ctx_tiles)
    pm = jnp.where(r == 0, 0.0, 1.0)
    nm = jnp.where(r == n - 1, 0.0, 1.0)
    buf_ref[0:HALO, :] = zp_ref[...].astype(F32) * pm
    buf_ref[HALO:HALO + tm, :] = zc_ref[...].astype(F32)
    buf_ref[HALO + tm:2 * HALO + tm, :] = zn_ref[...].astype(F32) * nm
    half = CONV_W // 2
    for cb in range(3 * GDN_HEADS):
        cs = slice(cb * LANE, (cb + 1) * LANE)
        acc = jnp.zeros((tm, LANE), F32)
        for j in range(CONV_W):
            acc = acc + buf_ref[HALO - half + j:HALO - half + j + tm, cs] * cw_ref[j:j + 1, cs]
        y = acc * _sigmoid(acc)
        if cb < 2 * GDN_HEADS:
            y = y * lax.rsqrt(jnp.sum(y * y, axis=-1, keepdims=True) + EPS)
            if cb < GDN_HEADS:
                y = y * GDN_DK ** -0.5
        o_ref[:, cs] = y.astype(BF16)


def _gdn_prep(z, conv_w, n_lat_rows, seq, ctx_len, tm=256):
    t = z.shape[0]
    w = 3 * GDN_HEADS * LANE
    hb = tm // HALO
    last = t // HALO - 1
    kern = functools.partial(_gdn_prep_kernel, tm=tm, lat_tiles=n_lat_rows // tm,
                             seq_tiles=seq // tm, ctx_tiles=ctx_len // tm)
    return pl.pallas_call(
        kern, name="gdn_prep",
        grid=(t // tm,),
        in_specs=[pl.BlockSpec((tm, w), lambda i: (i, 0)),
                  pl.BlockSpec((HALO, w), lambda i: (jnp.maximum(i * hb - 1, 0), 0)),
                  pl.BlockSpec((HALO, w), lambda i: (jnp.minimum((i + 1) * hb, last), 0)),
                  pl.BlockSpec((CONV_W, w), lambda i: (0, 0))],
        out_specs=pl.BlockSpec((tm, w), lambda i: (i, 0)),
        out_shape=jax.ShapeDtypeStruct((t, w), BF16),
        scratch_shapes=[pltpu.VMEM((tm + 2 * HALO, w), F32)],
        compiler_params=_cp(("parallel",)),
    )(z, z, z, conv_w)


def _scan_block(b, d, j, n_lat_blocks, n_ctx_blocks, batch):
    jc = jnp.where(d == 0, j, n_ctx_blocks - 1 - j)
    jl = jnp.where(d == 0, j - n_ctx_blocks, n_lat_blocks - 1 - (j - n_ctx_blocks))
    return jnp.where(j < n_ctx_blocks, batch * n_lat_blocks + b * n_ctx_blocks + jc, b * n_lat_blocks + jl)


def _order_masks(c, d):
    ii = lax.broadcasted_iota(jnp.int32, (c, c), 0)
    jj = lax.broadcasted_iota(jnp.int32, (c, c), 1)
    diff = (ii - jj) * (1 - 2 * d)
    return ii, jj, diff >= 0, diff > 0, diff <= 0


def _gdn_scan_kernel(qkv_ref, ba_ref, alog_ref, dtb_ref, o_ref, s_ref, *, nchunks):
    d = pl.program_id(1)
    j = pl.program_id(2)
    c = GDN_C

    @pl.when(j == 0)
    def _():
        s_ref[...] = jnp.zeros_like(s_ref)

    ii, jj, incl, strict, incl_t = _order_masks(c, d)
    eye = ii == jj
    blk = (ii // GDN_SUB) == (jj // GDN_SUB)
    lm = jnp.where(incl, 1.0, 0.0).astype(BF16)
    um = jnp.where(incl_t, 1.0, 0.0).astype(BF16)
    ones = jnp.ones((c, c), BF16)
    neg_a = -jnp.exp(alog_ref[...])
    dtb = dtb_ref[...]
    fwd = d == 0

    def pick(x, lane):
        return jnp.where(fwd, x[:, lane:lane + 1], x[:, lane + 8:lane + 9])

    def pick_row(x, lane):
        return jnp.where(fwd, x[lane:lane + 1, :], x[lane + 8:lane + 9, :])

    def chunk(ci, carry):
        cc = jnp.where(fwd, ci, nchunks - 1 - ci)
        rows = pl.ds(pl.multiple_of(cc * c, c), c)
        ba = ba_ref[rows, :].astype(F32)
        la_all = neg_a * _softplus(ba + dtb)
        beta_all = _sigmoid(ba)
        g_all = _dot01(lm, la_all)
        tot_all = _dot01(ones, la_all)
        gt_all = _dot01_tn(la_all, um)
        for h in range(GDN_HEADS):
            beta = pick(beta_all, h)
            gcol = pick(g_all, 4 + h)
            tot = pick(tot_all, 4 + h)
            grow = pick_row(gt_all, 4 + h)
            q = qkv_ref[rows, h * LANE:(h + 1) * LANE]
            k = qkv_ref[rows, (GDN_HEADS + h) * LANE:(GDN_HEADS + h + 1) * LANE]
            v = qkv_ref[rows, (2 * GDN_HEADS + h) * LANE:(2 * GDN_HEADS + h + 1) * LANE]
            qf, kf, vf = q.astype(F32), k.astype(F32), v.astype(F32)
            kb = kf * beta
            gram = _dot_nt(jnp.concatenate([kb, qf], axis=0).astype(BF16), k)
            dec = jnp.exp(jnp.where(incl, gcol - grow, -1e30))
            nmat = jnp.where(strict, gram[0:c] * dec, 0.0)
            q_att = (gram[c:2 * c] * dec).astype(BF16)
            eg = jnp.exp(gcol)
            dg = jnp.where(blk, nmat, 0.0).astype(BF16)
            off = jnp.where(blk, 0.0, nmat).astype(BF16)
            t = jnp.where(eye, 1.0, 0.0) - dg.astype(F32)
            p = _dot(dg, dg)
            for lvl in range(3):
                pb = p.astype(BF16)
                t = t + _dot(t.astype(BF16), pb)
                if lvl < 2:
                    p = _dot(pb, pb)
            tb = t.astype(BF16)
            m = _dot(tb, off).astype(BF16)
            x = _dot(tb, jnp.concatenate([vf * beta, kb * eg], axis=1).astype(BF16))
            m2 = _dot(m, m).astype(BF16)
            x = x - _dot(m, x.astype(BF16))
            x = x + _dot(m2, x.astype(BF16))
            u = x[:, 0:LANE]
            w = x[:, LANE:2 * LANE]
            s = s_ref[h]
            ws = _dot(jnp.concatenate([w, qf * eg], axis=0).astype(BF16), s.astype(BF16))
            v_new = (u - ws[0:c]).astype(BF16)
            o = ws[c:2 * c] + _dot(q_att, v_new)
            k_end = (kf * jnp.exp(tot - gcol)).astype(BF16)
            s_ref[h] = s * jnp.exp(tot[0:1, 0:1]) + _dot_tn(k_end, v_new)
            o_ref[0, rows, h * LANE:(h + 1) * LANE] = o.astype(BF16)
        return carry

    lax.fori_loop(0, nchunks, chunk, 0)


def _gdn_scan(qkv, z, a_log, dt_bias, batch, seq, ctx_len):
    t = qkv.shape[0]
    tb = SCAN_BLOCK
    nlb, ncb = seq // tb, ctx_len // tb
    w = 3 * GDN_HEADS * LANE

    def lanes(p):
        v = jnp.zeros((LANE,), F32)
        v = v.at[4:8].set(p[0].astype(F32)).at[12:16].set(p[1].astype(F32))
        return v.reshape(1, LANE)

    blk = functools.partial(_scan_block, n_lat_blocks=nlb, n_ctx_blocks=ncb, batch=batch)
    return pl.pallas_call(
        functools.partial(_gdn_scan_kernel, nchunks=tb // GDN_C), name="gdn_scan",
        grid=(batch, 2, nlb + ncb),
        in_specs=[pl.BlockSpec((tb, w), lambda b, d, j: (blk(b, d, j), 0)),
                  pl.BlockSpec((tb, LANE), lambda b, d, j: (blk(b, d, j), Z_BA // LANE)),
                  pl.BlockSpec((1, LANE), lambda b, d, j: (0, 0)),
                  pl.BlockSpec((1, LANE), lambda b, d, j: (0, 0))],
        out_specs=pl.BlockSpec((1, tb, GDN_HEADS * GDN_DV), lambda b, d, j: (d, blk(b, d, j), 0)),
        out_shape=jax.ShapeDtypeStruct((2, t, GDN_HEADS * GDN_DV), BF16),
        scratch_shapes=[pltpu.VMEM((GDN_HEADS, GDN_DK, GDN_DV), F32)],
        compiler_params=_cp(("parallel", "parallel", "arbitrary")),
    )(qkv, z, lanes(a_log), lanes(dt_bias))


def _hgrn_scan_kernel(q_ref, f_ref, i_ref, lb_ref, o_ref, s_ref, *, nchunks):
    d = pl.program_id(1)
    j = pl.program_id(2)
    c = HG_C
    fwd = d == 0

    @pl.when(j == 0)
    def _():
        s_ref[...] = jnp.zeros_like(s_ref)

    ii, jj, incl, _, _ = _order_masks(c, d)
    lm = jnp.where(incl, 1.0, 0.0).astype(BF16)
    ones = jnp.ones((c, c), BF16)
    eye = ii == jj
    ri = lax.broadcasted_iota(jnp.int32, (c, 1), 0)
    levels = []
    s = c // 2
    while s >= 1:
        later = ((ii & (2 * s - 1)) >= s) == fwd
        earlier_j = ((jj & (2 * s - 1)) >= s) != fwd
        same = (ii & ~(2 * s - 1)) == (jj & ~(2 * s - 1))
        ref_row = (ii & ~(2 * s - 1)) + (s - 1) + d
        sel = jnp.where(jj == ref_row, 1.0, 0.0).astype(BF16)
        is_q = ((ri & (2 * s - 1)) >= s) == fwd
        levels.append((sel, is_q, later & earlier_j & same))
        s //= 2

    lb = lb_ref[0]
    log_lb = jnp.log(lb)
    log_1m = jnp.log(1.0 - lb)

    def chunk(ci, carry):
        cc = jnp.where(fwd, ci, nchunks - 1 - ci)
        rows = pl.ds(pl.multiple_of(cc * c, c), c)
        fr = f_ref[rows, :].astype(F32)
        ls = jnp.minimum(fr, 0.0) - jnp.log(1.0 + jnp.exp(-jnp.abs(fr)))
        bb = log_1m + ls
        lf = jnp.maximum(log_lb, bb) + jnp.log(1.0 + jnp.exp(-jnp.abs(log_lb - bb)))
        kk = (1.0 - lb) / (1.0 + jnp.exp(fr))
        q = q_ref[rows, :].astype(F32) * HG_DK ** -0.5
        vv = i_ref[rows, :]
        bcum = _dot01(lm, lf)
        btot = _dot01(ones, lf)
        qd = (q * jnp.exp(bcum)).astype(BF16)
        ke = (kk * jnp.exp(btot - bcum)).astype(BF16)
        qb = q.astype(BF16)
        kkb = kk.astype(BF16)
        att = [jnp.where(eye, _dot_nt(qb[:, h * LANE:(h + 1) * LANE], kkb[:, h * LANE:(h + 1) * LANE]), 0.0)
               for h in range(HG_HEADS)]
        for sel, is_q, mask in levels:
            r = _dot01(sel, bcum)
            e = jnp.exp(jnp.where(is_q, bcum - r, r - bcum))
            x = (jnp.where(is_q, q, kk) * e).astype(BF16)
            for h in range(HG_HEADS):
                xh = x[:, h * LANE:(h + 1) * LANE]
                att[h] = att[h] + jnp.where(mask, _dot_nt(xh, xh), 0.0)
        outs = []
        for h in range(HG_HEADS):
            hs = slice(h * LANE, (h + 1) * LANE)
            st = s_ref[h]
            vh = vv[:, hs]
            outs.append(_dot_nt(qd[:, hs], st.astype(BF16)) + _dot(att[h].astype(BF16), vh))
            s_ref[h] = st * jnp.exp(btot[0:1, hs]) + _dot_tn(vh, ke[:, hs])
        o_ref[0, rows, 0:LANE] = (outs[0] + outs[1]).astype(BF16)
        o_ref[0, rows, LANE:2 * LANE] = (outs[2] + outs[3]).astype(BF16)
        return carry

    lax.fori_loop(0, nchunks, chunk, 0)


def _hgrn_scan(z, lb, batch, seq, ctx_len):
    t = z.shape[0]
    tb = SCAN_BLOCK
    nlb, ncb = seq // tb, ctx_len // tb
    w = HG_HEADS * HG_DK
    blk = functools.partial(_scan_block, n_lat_blocks=nlb, n_ctx_blocks=ncb, batch=batch)
    return pl.pallas_call(
        functools.partial(_hgrn_scan_kernel, nchunks=tb // HG_C), name="hgrn_scan",
        grid=(batch, 2, nlb + ncb),
        in_specs=[pl.BlockSpec((tb, w), lambda b, d, j: (blk(b, d, j), Z_HQ // w)),
                  pl.BlockSpec((tb, w), lambda b, d, j: (blk(b, d, j), Z_HFF // w + d)),
                  pl.BlockSpec((tb, w), lambda b, d, j: (blk(b, d, j), Z_HI // w)),
                  pl.BlockSpec((1, 1, w), lambda b, d, j: (d, 0, 0))],
        out_specs=pl.BlockSpec((1, tb, HG_HEADS * HG_DV), lambda b, d, j: (d, blk(b, d, j), 0)),
        out_shape=jax.ShapeDtypeStruct((2, t, HG_HEADS * HG_DV), BF16),
        scratch_shapes=[pltpu.VMEM((HG_HEADS, LANE, HG_DK), F32)],
        compiler_params=_cp(("parallel", "parallel", "arbitrary")),
    )(z, z, z, lb.reshape(2, 1, w))


def _mla_prep_kernel(cq_ref, ckv_ref, kr_ref, krp_ref, cos_ref, sin_ref, gq_ref, gkv_ref,
                     wqa_ref, wqb_ref, wk_ref, wv_ref, q_ref, k_ref, v_ref):
    cos = cos_ref[...]
    sin = sin_ref[...]
    cqn = _rms(cq_ref[...].astype(F32), gq_ref[...]).astype(BF16)
    qa = _dot(cqn, wqa_ref[...])
    qb = _dot(cqn, wqb_ref[...])
    scale = (MLA_NOPE + MLA_ROPE) ** -0.5
    ckvn = _rms(ckv_ref[...].astype(F32), gkv_ref[...]).astype(BF16)
    kn = _dot(ckvn, wk_ref[...])
    kr = kr_ref[...].astype(F32) * cos + krp_ref[...].astype(F32) * sin
    for h in range(MLA_HEADS):
        hs = slice(h * LANE, (h + 1) * LANE)
        q_ref[:, hs] = ((qa[:, hs] * cos + qb[:, hs] * sin) * scale).astype(BF16)
        k_ref[:, hs] = (kn[:, hs] + kr).astype(BF16)
    v_ref[...] = _dot(ckvn, wv_ref[...]).astype(BF16)


def _mla_prep(z, cos_t, sin_t, gq, gkv, wqa, wqb, wk, wv, n_lat_rows, seq, tm=256):
    t = z.shape[0]
    lat_tiles, seq_tiles = n_lat_rows // tm, seq // tm
    hw = MLA_HEADS * LANE

    def tab(i):
        return (jnp.where(i < lat_tiles, i % seq_tiles, seq_tiles), 0)

    full = lambda a: pl.BlockSpec(a.shape, lambda i: (0, 0))
    return pl.pallas_call(
        _mla_prep_kernel, name="mla_prep",
        grid=(t // tm,),
        in_specs=[pl.BlockSpec((tm, MLA_Q_RANK), lambda i: (i, Z_CQ // MLA_Q_RANK)),
                  pl.BlockSpec((tm, LANE), lambda i: (i, Z_CKV // LANE)),
                  pl.BlockSpec((tm, LANE), lambda i: (i, Z_KR // LANE)),
                  pl.BlockSpec((tm, LANE), lambda i: (i, Z_KRP // LANE)),
                  pl.BlockSpec((tm, LANE), tab),
                  pl.BlockSpec((tm, LANE), tab),
                  full(gq), full(gkv), full(wqa), full(wqb), full(wk), full(wv)],
        out_specs=[pl.BlockSpec((tm, hw), lambda i: (i, 0))] * 3,
        out_shape=[jax.ShapeDtypeStruct((t, hw), BF16)] * 3,
        compiler_params=_cp(("parallel",)),
    )(z, z, z, z, cos_t, sin_t, gq, gkv, wqa, wqb, wk, wv)


def _attn_kernel(q_ref, kl_ref, vl_ref, kc_ref, vc_ref, o_ref, *, lat_tiles):
    i = pl.program_id(1)

    def run(use_latent):
        halves = []
        for h in range(MLA_HEADS):
            hs = slice(h * LANE, (h + 1) * LANE)
            q = q_ref[:, hs]
            sc = _dot_nt(q, kc_ref[:, hs])
            m = jnp.max(sc, axis=-1, keepdims=True)
            if use_latent:
                sl = _dot_nt(q, kl_ref[:, hs])
                m = jnp.maximum(m, jnp.max(sl, axis=-1, keepdims=True))
            pc = jnp.exp(sc - m)
            den = jnp.sum(pc, axis=-1, keepdims=True)
            acc = _dot(pc.astype(BF16), vc_ref[:, hs])
            if use_latent:
                pl_ = jnp.exp(sl - m)
                den = den + jnp.sum(pl_, axis=-1, keepdims=True)
                acc = acc + _dot(pl_.astype(BF16), vl_ref[:, hs])
            halves.append(acc / den)
        o_ref[:, 0:LANE] = (halves[0] + halves[1]).astype(BF16)
        o_ref[:, LANE:2 * LANE] = (halves[2] + halves[3]).astype(BF16)

    @pl.when(i < lat_tiles)
    def _():
        run(True)

    @pl.when(i >= lat_tiles)
    def _():
        run(False)


def _attention(q, k, v, batch, seq, ctx_len, with_ctx, tq=256):
    t = q.shape[0]
    hw = MLA_HEADS * LANE
    lat_tiles = seq // tq
    ctx_q_tiles = ctx_len // tq if with_ctx else 0
    ctx_base_q = batch * lat_tiles
    ctx_base_k = batch * seq // ctx_len

    def qmap(b, i):
        return (jnp.where(i < lat_tiles, b * lat_tiles + i,
                          ctx_base_q + b * (ctx_len // tq) + (i - lat_tiles)), 0)

    return pl.pallas_call(
        functools.partial(_attn_kernel, lat_tiles=lat_tiles), name="mla_attn",
        grid=(batch, lat_tiles + ctx_q_tiles),
        in_specs=[pl.BlockSpec((tq, hw), qmap),
                  pl.BlockSpec((seq, hw), lambda b, i: (b, 0)),
                  pl.BlockSpec((seq, hw), lambda b, i: (b, 0)),
                  pl.BlockSpec((ctx_len, hw), lambda b, i: (ctx_base_k + b, 0)),
                  pl.BlockSpec((ctx_len, hw), lambda b, i: (ctx_base_k + b, 0))],
        out_specs=pl.BlockSpec((tq, MLA_HEADS * MLA_DV), qmap),
        out_shape=jax.ShapeDtypeStruct((t, MLA_HEADS * MLA_DV), BF16),
        compiler_params=_cp(("parallel", "arbitrary")),
    )(q, k, v, k, v)


def _out_proj_kernel(h_ref, mod_ref, og0_ref, og1_ref, gg_ref, ym_ref, oh0_ref, oh1_ref, hg_ref,
                     gn_ref, hn_ref, wo_ref, gpost_ref, o_ref):
    tm = h_ref.shape[0]
    og = og0_ref[0].astype(F32) + og1_ref[0].astype(F32)
    gate = gg_ref[...].astype(F32)
    acc = jnp.zeros((tm, D_MODEL), F32)
    for h in range(GDN_HEADS):
        hs = slice(h * LANE, (h + 1) * LANE)
        g = gate[:, hs]
        y = _rms(og[:, hs], gn_ref[...]) * (g * _sigmoid(g))
        acc = acc + _dot(y.astype(BF16), wo_ref[h * LANE:(h + 1) * LANE, :])
    base = GDN_HEADS * GDN_DV
    acc = acc + _dot(ym_ref[...], wo_ref[base:base + MLA_HEADS * MLA_DV, :])
    base += MLA_HEADS * MLA_DV
    oh = oh0_ref[0].astype(F32) + oh1_ref[0].astype(F32)
    hgate = hg_ref[...].astype(F32)
    lane = lax.broadcasted_iota(jnp.int32, (1, LANE), 1)
    lo = lane < HG_DV
    for p in range(HG_HEADS // 2):
        ps = slice(p * LANE, (p + 1) * LANE)
        x = oh[:, ps]
        sq = x * x
        ms_lo = jnp.sum(jnp.where(lo, sq, 0.0), axis=-1, keepdims=True) / HG_DV
        ms_hi = jnp.sum(jnp.where(lo, 0.0, sq), axis=-1, keepdims=True) / HG_DV
        ms = jnp.where(lo, ms_lo, ms_hi)
        g = hgate[:, ps]
        y = x * lax.rsqrt(ms + EPS) * hn_ref[...] * (g * _sigmoid(g))
        acc = acc + _dot(y.astype(BF16), wo_ref[base + p * LANE:base + (p + 1) * LANE, :])
    m = mod_ref[0]
    o_ref[...] = h_ref[...] + m[:, 2 * D_MODEL:3 * D_MODEL] * _rms(acc, gpost_ref[...])


def _out_proj(h, mods, og, z, ym, oh, gdn_norm, hgrn_norm, wo, gpost, n_rows, n_lat_rows, seq, tm=512):
    midx = lambda i: (_mod_index(i, tm, n_lat_rows, seq), 0, 0)
    gw, hw = GDN_HEADS * GDN_DV, HG_HEADS * HG_DV
    full = lambda a: pl.BlockSpec(a.shape, lambda i: (0, 0))
    hn2 = jnp.concatenate([hgrn_norm, hgrn_norm]).reshape(1, LANE)
    gn = gdn_norm.reshape(1, LANE)
    return pl.pallas_call(
        _out_proj_kernel, name="out_proj",
        grid=(n_rows // tm,),
        in_specs=[pl.BlockSpec((tm, D_MODEL), lambda i: (i, 0)),
                  pl.BlockSpec((1, 1, 6 * D_MODEL), midx),
                  pl.BlockSpec((1, tm, gw), lambda i: (0, i, 0)),
                  pl.BlockSpec((1, tm, gw), lambda i: (1, i, 0)),
                  pl.BlockSpec((tm, gw), lambda i: (i, Z_GG // gw)),
                  pl.BlockSpec((tm, MLA_HEADS * MLA_DV), lambda i: (i, 0)),
                  pl.BlockSpec((1, tm, hw), lambda i: (0, i, 0)),
                  pl.BlockSpec((1, tm, hw), lambda i: (1, i, 0)),
                  pl.BlockSpec((tm, hw), lambda i: (i, Z_HGATE // hw)),
                  full(gn), full(hn2), full(wo),
                  pl.BlockSpec((1, D_MODEL), lambda i: (0, 0))],
        out_specs=pl.BlockSpec((tm, D_MODEL), lambda i: (i, 0)),
        out_shape=jax.ShapeDtypeStruct((n_rows, D_MODEL), F32),
        compiler_params=_cp(("parallel",)),
    )(h, mods, og, og, z, ym, oh, oh, z, gn, hn2, wo, gpost.reshape(1, D_MODEL))


def _mlp_kernel(h_ref, mod_ref, gpre_ref, gpost_ref, w1_ref, w2_ref, o_ref, hm_ref, acc_ref):
    f = pl.program_id(1)

    @pl.when(f == 0)
    def _():
        m = mod_ref[0]
        y = _rms(h_ref[...], gpre_ref[...])
        hm_ref[...] = (y * (1.0 + m[:, 4 * D_MODEL:5 * D_MODEL]) + m[:, 3 * D_MODEL:4 * D_MODEL]).astype(BF16)
        acc_ref[...] = jnp.zeros_like(acc_ref)

    a = jnp.maximum(_dot(hm_ref[...], w1_ref[...]), 0.0)
    acc_ref[...] += _dot((a * a).astype(BF16), w2_ref[...])

    @pl.when(f == pl.num_programs(1) - 1)
    def _():
        m = mod_ref[0]
        o_ref[...] = h_ref[...] + m[:, 5 * D_MODEL:6 * D_MODEL] * _rms(acc_ref[...], gpost_ref[...])


def _mlp(h, mods, gpre, gpost, w1, w2, n_rows, n_lat_rows, seq, tm=1024, tf=1024):
    midx = lambda i, f: (_mod_index(i, tm, n_lat_rows, seq), 0, 0)
    return pl.pallas_call(
        _mlp_kernel, name="mlp",
        grid=(n_rows // tm, D_FF // tf),
        in_specs=[pl.BlockSpec((tm, D_MODEL), lambda i, f: (i, 0)),
                  pl.BlockSpec((1, 1, 6 * D_MODEL), midx),
                  pl.BlockSpec((1, D_MODEL), lambda i, f: (0, 0)),
                  pl.BlockSpec((1, D_MODEL), lambda i, f: (0, 0)),
                  pl.BlockSpec((D_MODEL, tf), lambda i, f: (0, f)),
                  pl.BlockSpec((tf, D_MODEL), lambda i, f: (f, 0))],
        out_specs=pl.BlockSpec((tm, D_MODEL), lambda i, f: (i, 0)),
        out_shape=jax.ShapeDtypeStruct((n_rows, D_MODEL), F32),
        scratch_shapes=[pltpu.VMEM((tm, D_MODEL), BF16), pltpu.VMEM((tm, D_MODEL), F32)],
        compiler_params=_cp(("parallel", "arbitrary")),
    )(h, mods, gpre.reshape(1, D_MODEL), gpost.reshape(1, D_MODEL), w1, w2)


def _arrange_w_in(w):
    (gq, gk, gv, gg, gb, ga, cq, ckv, kr, hq, hff, hfb, hi, hgate) = jnp.split(
        w, list(itertools_accumulate(IN_SIZES))[:-1], axis=1)
    zc = lambda n: jnp.zeros((w.shape[0], n), w.dtype)
    ba = jnp.concatenate([gb[:, 0:4], ga[:, 0:4], gb[:, 4:8], ga[:, 4:8], zc(LANE - 16)], axis=1)
    half = MLA_ROPE // 2
    krb = jnp.concatenate([zc(MLA_NOPE), kr, zc(LANE - MLA_NOPE - MLA_ROPE)], axis=1)
    krp = jnp.concatenate([zc(MLA_NOPE), -kr[:, half:], kr[:, :half], zc(LANE - MLA_NOPE - MLA_ROPE)], axis=1)
    his = []
    for h in range(HG_HEADS):
        blk = hi[:, h * HG_DV:(h + 1) * HG_DV]
        his += [blk, zc(HG_DV)] if h % 2 == 0 else [zc(HG_DV), blk]
    out = jnp.concatenate([gq, gk, gv, gg, hq, hff, hfb] + his + [hgate, cq, ba, ckv, krb, krp], axis=1)
    return out.astype(BF16)


def itertools_accumulate(sizes):
    tot = 0
    for s in sizes:
        tot += s
        yield tot


def _arrange_mla(w_uq, w_ukv):
    half = MLA_ROPE // 2
    dq = MLA_NOPE + MLA_ROPE
    zq = lambda n: jnp.zeros((w_uq.shape[0], n), w_uq.dtype)
    zk = lambda n: jnp.zeros((w_ukv.shape[0], n), w_ukv.dtype)
    qa, qb, wk, wv = [], [], [], []
    for h in range(MLA_HEADS):
        nope = w_uq[:, h * dq:h * dq + MLA_NOPE]
        rope = w_uq[:, h * dq + MLA_NOPE:(h + 1) * dq]
        qa += [nope, rope, zq(LANE - dq)]
        qb += [zq(MLA_NOPE), -rope[:, half:], rope[:, :half], zq(LANE - dq)]
        kv = w_ukv[:, h * (MLA_NOPE + MLA_DV):(h + 1) * (MLA_NOPE + MLA_DV)]
        wk += [kv[:, :MLA_NOPE], zk(LANE - MLA_NOPE)]
        wv += [kv[:, MLA_NOPE:], zk(MLA_DV)] if h % 2 == 0 else [zk(MLA_DV), kv[:, MLA_NOPE:]]
    cat = lambda xs: jnp.concatenate(xs, axis=1).astype(BF16)
    return cat(qa), cat(qb), cat(wk), cat(wv)


def _rope_tables(seq, tile):
    per_axis = MLA_ROPE // 2
    inv = ROPE_BASE ** (-jnp.arange(0, per_axis, 2, dtype=F32) / per_axis)
    rows = seq // GRID_W
    row = jnp.repeat(jnp.arange(rows, dtype=F32), GRID_W)
    col = jnp.tile(jnp.arange(GRID_W, dtype=F32), rows)
    ang = jnp.concatenate([row[:, None] * inv, col[:, None] * inv], axis=-1)
    cos, sin = jnp.cos(ang), jnp.sin(ang)
    one = jnp.ones((seq, MLA_NOPE), F32)
    pad1 = jnp.ones((seq, LANE - MLA_NOPE - MLA_ROPE), F32)
    cos_t = jnp.concatenate([one, cos, cos, pad1], axis=1)
    sin_t = jnp.concatenate([0 * one, sin, sin, 0 * pad1], axis=1)
    cos_t = jnp.concatenate([cos_t, jnp.ones((tile, LANE), F32)], axis=0)
    sin_t = jnp.concatenate([sin_t, jnp.zeros((tile, LANE), F32)], axis=0)
    return cos_t, sin_t


def kernel(x, c, ctx, c_ctx, w_ada, b_ada, norm_mix_pre, norm_mix_post, norm_mlp_pre, norm_mlp_post,
           w_in, w_out, gdn_conv, gdn_a_log, gdn_dt_bias, gdn_norm, mla_q_norm, mla_kv_norm,
           mla_w_uq, mla_w_ukv, hgrn_lb_logits, hgrn_norm, w_mlp1, w_mlp2):
    batch, seq, _ = x.shape
    ctx_len = ctx.shape[1]
    depth = w_in.shape[0]
    n_lat = batch * seq
    n_all = n_lat + batch * ctx_len
    assert seq % 1024 == 0 and ctx_len % SCAN_BLOCK == 0 and (batch * ctx_len) % 1024 == 0

    cos_t, sin_t = _rope_tables(seq, 256)
    lb_cum = jnp.cumsum(jax.nn.softmax(hgrn_lb_logits.astype(F32), axis=0), axis=0)
    lower_bounds = lb_cum - lb_cum[0]
    cvec = jnp.concatenate([c_ctx[None, :], c, jnp.zeros((16 - 1 - batch, D_MODEL), F32)], axis=0)

    h = jnp.concatenate([x.reshape(n_lat, D_MODEL), ctx.reshape(batch * ctx_len, D_MODEL)], axis=0)
    for l in range(depth):
        last = l == depth - 1
        n_rows = n_lat if last else n_all
        mods = _ada(cvec, w_ada[l], b_ada[l])[:1 + batch].reshape(1 + batch, 1, 6 * D_MODEL)
        z = _in_proj(h, mods, norm_mix_pre[l], _arrange_w_in(w_in[l]), n_lat, seq)
        qkv = _gdn_prep(z, gdn_conv[l], n_lat, seq, ctx_len)
        og = _gdn_scan(qkv, z, gdn_a_log[l], gdn_dt_bias[l], batch, seq, ctx_len)
        wqa, wqb, wk, wv = _arrange_mla(mla_w_uq[l], mla_w_ukv[l])
        q, k, v = _mla_prep(z, cos_t, sin_t, mla_q_norm[l].reshape(1, -1), mla_kv_norm[l].reshape(1, -1),
                            wqa, wqb, wk, wv, n_lat, seq)
        ym = _attention(q, k, v, batch, seq, ctx_len, with_ctx=not last)
        oh = _hgrn_scan(z, lower_bounds[l], batch, seq, ctx_len)
        h = _out_proj(h, mods, og, z, ym, oh, gdn_norm[l], hgrn_norm[l], w_out[l].astype(BF16),
                      norm_mix_post[l], n_rows, n_lat, seq)
        h = _mlp(h, mods, norm_mlp_pre[l], norm_mlp_post[l], w_mlp1[l].astype(BF16), w_mlp2[l].astype(BF16),
                 n_rows, n_lat, seq)
    return h[:n_lat].reshape(batch, seq, D_MODEL)
```

```python
import functools

import jax
import jax.numpy as jnp
from jax import lax
from jax.experimental import pallas as pl
from jax.experimental.pallas import tpu as pltpu

F32 = jnp.float32
BF16 = jnp.bfloat16

D_MODEL = 1024
D_FF = 4 * D_MODEL
GRID_W = 64
EPS = 1e-6
ROPE_BASE = 10000.0
GDN_HEADS, GDN_DK, GDN_DV, CONV_W = 4, 128, 128, 5
MLA_HEADS, MLA_Q_RANK, MLA_KV_RANK, MLA_NOPE, MLA_ROPE, MLA_DV = 4, 256, 128, 64, 32, 64
HG_HEADS, HG_DK, HG_DV = 4, 128, 64
IN_SIZES = (512, 512, 512, 512, 8, 8, 256, 128, 32, 512, 512, 512, 256, 256)

LANE = 128
SUBLANE = 8
VMEM_LIMIT = 56 * 1024 * 1024

Z_QKV, Z_GG, Z_HQ, Z_HFF, Z_HFB, Z_HI, Z_HGATE, Z_CQ, Z_BA, Z_CKV, Z_KR, Z_KRP, NZ = (
    0, 1536, 2048, 2560, 3072, 3584, 4096, 4352, 4608, 4736, 4864, 4992, 5120)

SCAN_BLOCK = 256
GDN_C = 64
GDN_SUB = 16
GDN_UNROLL = 2
HG_C = 64
HG_UNROLL = 2
GW = GDN_HEADS * GDN_DV
HW = HG_HEADS * HG_DK


def _cp(sem, vmem=VMEM_LIMIT):
    return pltpu.CompilerParams(dimension_semantics=sem, vmem_limit_bytes=vmem)


def _dot(a, b):
    return jnp.dot(a, b, preferred_element_type=F32)


def _dot_nt(a, b):
    return lax.dot_general(a, b, (((1,), (1,)), ((), ())), preferred_element_type=F32)


def _dot_tn(a, b):
    return lax.dot_general(a, b, (((0,), (0,)), ((), ())), preferred_element_type=F32)


def _split3(x):
    hi = x.astype(BF16)
    r = x - hi.astype(F32)
    mid = r.astype(BF16)
    lo = (r - mid.astype(F32)).astype(BF16)
    return hi, mid, lo


def _dot01(m01, x):
    hi, mid, lo = _split3(x)
    return _dot(m01, hi) + _dot(m01, mid) + _dot(m01, lo)


def _dot01_tn(x, m01):
    hi, mid, lo = _split3(x)
    return _dot_tn(hi, m01) + _dot_tn(mid, m01) + _dot_tn(lo, m01)


def _sigmoid(x):
    return 1.0 / (1.0 + jnp.exp(-x))


def _softplus(x):
    return jnp.maximum(x, 0.0) + jnp.log(1.0 + jnp.exp(-jnp.abs(x)))


def _rms(x, g):
    ms = jnp.mean(x * x, axis=-1, keepdims=True)
    return x * lax.rsqrt(ms + EPS) * g


def _b01(mask):
    return jnp.where(mask, 1.0, 0.0).astype(BF16)


def _ada_kernel(c_ref, w_ref, b_ref, o_ref):
    c = c_ref[...]
    s = c * _sigmoid(c)
    o_ref[...] = jnp.dot(s, w_ref[...], preferred_element_type=F32,
                         precision=lax.Precision.HIGHEST) + b_ref[...]


def _ada(cvec, w, b):
    n = w.shape[1]
    tn = 1024
    return pl.pallas_call(
        _ada_kernel, name="ada_mod",
        grid=(n // tn,),
        in_specs=[pl.BlockSpec(cvec.shape, lambda j: (0, 0)),
                  pl.BlockSpec((w.shape[0], tn), lambda j: (0, j)),
                  pl.BlockSpec((1, tn), lambda j: (0, j))],
        out_specs=pl.BlockSpec((cvec.shape[0], tn), lambda j: (0, j)),
        out_shape=jax.ShapeDtypeStruct((cvec.shape[0], n), F32),
        compiler_params=_cp(("parallel",)),
    )(cvec, w, b.reshape(1, n))


def _in_proj_kernel(h_ref, mod_ref, g_ref, w_ref, z_ref, hm_ref):
    @pl.when(pl.program_id(1) == 0)
    def _():
        m = mod_ref[0]
        y = _rms(h_ref[...], g_ref[...])
        hm_ref[...] = (y * (1.0 + m[:, D_MODEL:2 * D_MODEL]) + m[:, 0:D_MODEL]).astype(BF16)

    z_ref[...] = _dot(hm_ref[...], w_ref[...]).astype(BF16)


def _mod_index(i, tm, n_lat_rows, seq):
    lat_tiles = n_lat_rows // tm
    return jnp.where(i < lat_tiles, 1 + i // (seq // tm), 0)


def _in_proj(h, mods, g, w, n_lat_rows, seq, tm=1024, tn=512):
    t = h.shape[0]
    return pl.pallas_call(
        _in_proj_kernel, name="in_proj",
        grid=(t // tm, NZ // tn),
        in_specs=[pl.BlockSpec((tm, D_MODEL), lambda i, j: (i, 0)),
                  pl.BlockSpec((1, 1, 6 * D_MODEL), lambda i, j: (_mod_index(i, tm, n_lat_rows, seq), 0, 0)),
                  pl.BlockSpec((1, D_MODEL), lambda i, j: (0, 0)),
                  pl.BlockSpec((D_MODEL, tn), lambda i, j: (0, j))],
        out_specs=pl.BlockSpec((tm, tn), lambda i, j: (i, j)),
        out_shape=jax.ShapeDtypeStruct((t, NZ), BF16),
        scratch_shapes=[pltpu.VMEM((tm, D_MODEL), BF16)],
        compiler_params=_cp(("parallel", "arbitrary")),
    )(h, mods, g.reshape(1, D_MODEL), w)


HALO = 16


def _gdn_prep_kernel(zc_ref, zp_ref, zn_ref, cw_ref, o_ref, buf_ref, *, tm, lat_tiles, seq_tiles, ctx_tiles):
    i = pl.program_id(0)
    r = jnp.where(i < lat_tiles, i % seq_tiles, (i - lat_tiles) % ctx_tiles)
    n = jnp.where(i < lat_tiles, seq_tiles, ctx_tiles)
    pm = jnp.where(r == 0, 0.0, 1.0)
    nm = jnp.where(r == n - 1, 0.0, 1.0)
    buf_ref[0:HALO, :] = zp_ref[...].astype(F32) * pm
    buf_ref[HALO:HALO + tm, :] = zc_ref[...].astype(F32)
    buf_ref[HALO + tm:2 * HALO + tm, :] = zn_ref[...].astype(F32) * nm
    half = CONV_W // 2
    for cb in range(3 * GDN_HEADS):
        cs = slice(cb * LANE, (cb + 1) * LANE)
        acc = jnp.zeros((tm, LANE), F32)
        for j in range(CONV_W):
            acc = acc + buf_ref[HALO - half + j:HALO - half + j + tm, cs] * cw_ref[j:j + 1, cs]
        y = acc * _sigmoid(acc)
        if cb < 2 * GDN_HEADS:
            y = y * lax.rsqrt(jnp.sum(y * y, axis=-1, keepdims=True) + EPS)
            if cb < GDN_HEADS:
                y = y * GDN_DK ** -0.5
        o_ref[:, cs] = y.astype(BF16)


def _gdn_prep(z, conv_w, n_lat_rows, seq, ctx_len, tm=256):
    t = z.shape[0]
    w = 3 * GDN_HEADS * LANE
    hb = tm // HALO
    last = t // HALO - 1
    kern = functools.partial(_gdn_prep_kernel, tm=tm, lat_tiles=n_lat_rows // tm,
                             seq_tiles=seq // tm, ctx_tiles=ctx_len // tm)
    return pl.pallas_call(
        kern, name="gdn_prep",
        grid=(t // tm,),
        in_specs=[pl.BlockSpec((tm, w), lambda i: (i, 0)),
                  pl.BlockSpec((HALO, w), lambda i: (jnp.maximum(i * hb - 1, 0), 0)),
                  pl.BlockSpec((HALO, w), lambda i: (jnp.minimum((i + 1) * hb, last), 0)),
                  pl.BlockSpec((CONV_W, w), lambda i: (0, 0))],
        out_specs=pl.BlockSpec((tm, w), lambda i: (i, 0)),
        out_shape=jax.ShapeDtypeStruct((t, w), BF16),
        scratch_shapes=[pltpu.VMEM((tm + 2 * HALO, w), F32)],
        compiler_params=_cp(("parallel",)),
    )(z, z, z, conv_w)


def _scan_block(b, d, j, n_lat_blocks, n_ctx_blocks, batch):
    jc = j if d == 0 else n_ctx_blocks - 1 - j
    jl = j - n_ctx_blocks if d == 0 else n_lat_blocks - 1 - (j - n_ctx_blocks)
    return jnp.where(j < n_ctx_blocks, batch * n_lat_blocks + b * n_ctx_blocks + jc, b * n_lat_blocks + jl)


def _iotas(c):
    return lax.broadcasted_iota(jnp.int32, (c, c), 0), lax.broadcasted_iota(jnp.int32, (c, c), 1)


def _gdn_chunk_kernel(qkv_ref, ba_ref, alog_ref, dtb_ref, x0_ref, x1_ref, e0_ref, e1_ref, *, nchunks):
    c = GDN_C
    ii, jj = _iotas(c)
    eye = ii == jj
    blk = (ii // GDN_SUB) == (jj // GDN_SUB)
    low, up = ii >= jj, ii <= jj
    incl = (low, up)
    strict = (ii > jj, ii < jj)
    lm_all = jnp.concatenate([_b01(low), _b01(up), jnp.ones((c, c), BF16)], axis=0)
    um_all = jnp.concatenate([_b01(up), _b01(low)], axis=1)
    neg_a = -jnp.exp(alog_ref[...])
    dtb = dtb_ref[...]
    x_refs, e_refs = (x0_ref, x1_ref), (e0_ref, e1_ref)

    ident = jnp.where(eye, 1.0, 0.0)

    def group(gi, carry):
        chunks = [gi * GDN_UNROLL + u for u in range(GDN_UNROLL)]
        rows = [pl.ds(pl.multiple_of(ci * c, c), c) for ci in chunks]
        erows = [pl.ds(pl.multiple_of(ci * SUBLANE, SUBLANE), SUBLANE) for ci in chunks]
        ba = [ba_ref[r, :].astype(F32) for r in rows]
        la_all = [neg_a * _softplus(x + dtb) for x in ba]
        beta_all = [_sigmoid(x) for x in ba]
        parts = [_split3(x) for x in la_all]
        g3 = [[_dot(lm_all, pt) for pt in p3] for p3 in parts]
        gt = [[_dot_tn(pt, um_all) for pt in p3] for p3 in parts]
        g3 = [a + b + cc_ for a, b, cc_ in g3]
        gt = [a + b + cc_ for a, b, cc_ in gt]
        uh = [(u, h) for u in range(GDN_UNROLL) for h in range(GDN_HEADS)]
        q = [qkv_ref[rows[u], h * LANE:(h + 1) * LANE] for u, h in uh]
        k = [qkv_ref[rows[u], (GDN_HEADS + h) * LANE:(GDN_HEADS + h + 1) * LANE] for u, h in uh]
        v = [qkv_ref[rows[u], (2 * GDN_HEADS + h) * LANE:(2 * GDN_HEADS + h + 1) * LANE] for u, h in uh]
        gram = [_dot_nt(jnp.concatenate([k[i], q[i]], axis=0), k[i]) for i in range(len(uh))]
        ch = [(i, u, h, d) for i, (u, h) in enumerate(uh) for d in range(2)]
        beta, gcol, tot, q_att, dg, off, eg = [], [], [], [], [], [], []
        for i, u, h, d in ch:
            lb_, la_ = h + 8 * d, 4 + h + 8 * d
            beta.append(beta_all[u][:, lb_:lb_ + 1])
            gcol.append(g3[u][d * c:(d + 1) * c, la_:la_ + 1])
            tot.append(g3[u][2 * c:3 * c, la_:la_ + 1])
            grow = gt[u][la_:la_ + 1, d * c:(d + 1) * c]
            dec = jnp.exp(jnp.where(incl[d], gcol[-1] - grow, -1e30))
            nmat = jnp.where(strict[d], (beta[-1] * gram[i][0:c]) * dec, 0.0)
            q_att.append((gram[i][c:2 * c] * dec).astype(BF16))
            dg.append(jnp.where(blk, nmat, 0.0).astype(BF16))
            off.append(jnp.where(blk, 0.0, nmat).astype(BF16))
            eg.append(jnp.exp(gcol[-1]))
        n = len(ch)
        t = [ident - dg[j].astype(F32) for j in range(n)]
        p = [_dot(dg[j], dg[j]).astype(BF16) for j in range(n)]
        for lvl in range(3):
            t = [t[j] + _dot(t[j].astype(BF16), p[j]) for j in range(n)]
            if lvl < 2:
                p = [_dot(p[j], p[j]).astype(BF16) for j in range(n)]
        tb = [x.astype(BF16) for x in t]
        m = [_dot(tb[j], off[j]).astype(BF16) for j in range(n)]
        rhs = []
        for j, (i, u, h, d) in enumerate(ch):
            kb = k[i].astype(F32) * beta[j]
            rhs.append(jnp.concatenate([v[i].astype(F32) * beta[j], kb * eg[j]], axis=1).astype(BF16))
        x = [_dot(tb[j], rhs[j]) for j in range(n)]
        m2 = [_dot(m[j], m[j]).astype(BF16) for j in range(n)]
        x = [x[j] - _dot(m[j], x[j].astype(BF16)) for j in range(n)]
        xb = [(x[j] + _dot(m2[j], x[j].astype(BF16))).astype(BF16) for j in range(n)]
        qa = [_dot(q_att[j], xb[j]) for j in range(n)]
        for j, (i, u, h, d) in enumerate(ch):
            xr, r = x_refs[d], rows[u]
            xr[r, 0 * GW + h * LANE:0 * GW + (h + 1) * LANE] = xb[j][:, 0:LANE]
            xr[r, 1 * GW + h * LANE:1 * GW + (h + 1) * LANE] = xb[j][:, LANE:2 * LANE]
            xr[r, 2 * GW + h * LANE:2 * GW + (h + 1) * LANE] = (
                q[i].astype(F32) * eg[j] - qa[j][:, LANE:2 * LANE]).astype(BF16)
            xr[r, 3 * GW + h * LANE:3 * GW + (h + 1) * LANE] = (
                k[i].astype(F32) * jnp.exp(tot[j] - gcol[j])).astype(BF16)
            xr[r, 4 * GW + h * LANE:4 * GW + (h + 1) * LANE] = qa[j][:, 0:LANE].astype(BF16)
            e_refs[d][erows[u], h * LANE:(h + 1) * LANE] = jnp.broadcast_to(
                jnp.exp(tot[j][0:SUBLANE, :]), (SUBLANE, LANE))
        return carry

    lax.fori_loop(0, nchunks // GDN_UNROLL, group, 0)


def _gdn_chunk(qkv, z, a_log, dt_bias, tb=SCAN_BLOCK):
    t = qkv.shape[0]
    w = 3 * GDN_HEADS * LANE
    nch = tb // GDN_C

    def lanes(p):
        v = jnp.zeros((LANE,), F32)
        v = v.at[4:8].set(p[0].astype(F32)).at[12:16].set(p[1].astype(F32))
        return v.reshape(1, LANE)

    xs = jax.ShapeDtypeStruct((t, 5 * GW), BF16)
    es = jax.ShapeDtypeStruct((t // GDN_C * SUBLANE, GW), F32)
    xspec = pl.BlockSpec((tb, 5 * GW), lambda i: (i, 0))
    espec = pl.BlockSpec((nch * SUBLANE, GW), lambda i: (i, 0))
    return pl.pallas_call(
        functools.partial(_gdn_chunk_kernel, nchunks=nch), name="gdn_chunk",
        grid=(t // tb,),
        in_specs=[pl.BlockSpec((tb, w), lambda i: (i, 0)),
                  pl.BlockSpec((tb, LANE), lambda i: (i, Z_BA // LANE)),
                  pl.BlockSpec((1, LANE), lambda i: (0, 0)),
                  pl.BlockSpec((1, LANE), lambda i: (0, 0))],
        out_specs=[xspec, xspec, espec, espec],
        out_shape=[xs, xs, es, es],
        compiler_params=_cp(("parallel",)),
    )(qkv, z, lanes(a_log), lanes(dt_bias))


def _gdn_state_kernel(xf_ref, xb_ref, ef_ref, eb_ref, of_ref, ob_ref, s_ref, *, nchunks):
    c = GDN_C

    @pl.when(pl.program_id(1) == 0)
    def _():
        s_ref[...] = jnp.zeros_like(s_ref)

    dirs = ((xf_ref, ef_ref, of_ref), (xb_ref, eb_ref, ob_ref))
    state = [[s_ref[d, h] for h in range(GDN_HEADS)] for d in range(2)]
    dh = [(d, h) for d in range(2) for h in range(GDN_HEADS)]
    col = lambda g, h: slice(g * GW + h * LANE, g * GW + (h + 1) * LANE)
    for ci in range(nchunks):
        cc = [ci, nchunks - 1 - ci]
        rows = [slice(cc[d] * c, (cc[d] + 1) * c) for d in range(2)]
        ws = [_dot(jnp.concatenate([dirs[d][0][rows[d], col(1, h)], dirs[d][0][rows[d], col(2, h)]], axis=0),
                   state[d][h].astype(BF16)) for d, h in dh]
        v_new = [(dirs[d][0][rows[d], col(0, h)].astype(F32) - ws[i][0:c]).astype(BF16) for i, (d, h) in enumerate(dh)]
        upd = [_dot_tn(dirs[d][0][rows[d], col(3, h)], v_new[i]) for i, (d, h) in enumerate(dh)]
        for i, (d, h) in enumerate(dh):
            x_ref, e_ref, o_ref = dirs[d]
            o_ref[rows[d], h * LANE:(h + 1) * LANE] = (
                x_ref[rows[d], col(4, h)].astype(F32) + ws[i][c:2 * c]).astype(BF16)
            decay = e_ref[cc[d] * SUBLANE:cc[d] * SUBLANE + 1, h * LANE:(h + 1) * LANE]
            state[d][h] = state[d][h] * decay + upd[i]
    for d in range(2):
        for h in range(GDN_HEADS):
            s_ref[d, h] = state[d][h]


def _gdn_state(x0, x1, e0, e1, batch, seq, ctx_len, tb=SCAN_BLOCK):
    t = x0.shape[0]
    nlb, ncb = seq // tb, ctx_len // tb
    nch = tb // GDN_C
    blk = functools.partial(_scan_block, n_lat_blocks=nlb, n_ctx_blocks=ncb, batch=batch)
    fwd = lambda b, j: (blk(b, 0, j), 0)
    bwd = lambda b, j: (blk(b, 1, j), 0)
    os_ = jax.ShapeDtypeStruct((t, GW), BF16)
    return pl.pallas_call(
        functools.partial(_gdn_state_kernel, nchunks=nch), name="gdn_state",
        grid=(batch, nlb + ncb),
        in_specs=[pl.BlockSpec((tb, 5 * GW), fwd), pl.BlockSpec((tb, 5 * GW), bwd),
                  pl.BlockSpec((nch * SUBLANE, GW), fwd), pl.BlockSpec((nch * SUBLANE, GW), bwd)],
        out_specs=[pl.BlockSpec((tb, GW), fwd), pl.BlockSpec((tb, GW), bwd)],
        out_shape=[os_, os_],
        scratch_shapes=[pltpu.VMEM((2, GDN_HEADS, GDN_DK, GDN_DV), F32)],
        compiler_params=_cp(("parallel", "arbitrary")),
    )(x0, x1, e0, e1)


def _hgrn_chunk_kernel(q_ref, ff_ref, fb_ref, i_ref, lb_ref, oi0_ref, oi1_ref, qd0_ref, qd1_ref,
                       ke0_ref, ke1_ref, e0_ref, e1_ref, *, nchunks):
    c = HG_C
    ii, jj = _iotas(c)
    ri = lax.broadcasted_iota(jnp.int32, (c, 1), 0)
    eye = ii == jj
    f_refs = (ff_ref, fb_ref)
    outs = ((oi0_ref, qd0_ref, ke0_ref, e0_ref), (oi1_ref, qd1_ref, ke1_ref, e1_ref))

    stacks, levels = [], []
    for d in range(2):
        incl = (ii >= jj) if d == 0 else (ii <= jj)
        mats, lv = [_b01(incl), jnp.ones((c, c), BF16)], []
        s = c // 2
        while s >= 1:
            base = ii & ~(2 * s - 1)
            upper_i = (ii & (2 * s - 1)) >= s
            upper_j = (jj & (2 * s - 1)) >= s
            later_i = upper_i if d == 0 else ~upper_i
            earlier_j = ~upper_j if d == 0 else upper_j
            ref_row = base + (s - 1) + d
            mats.append(_b01((ref_row >= jj) if d == 0 else (ref_row <= jj)))
            upper_r = (ri & (2 * s - 1)) >= s
            is_q = upper_r if d == 0 else ~upper_r
            lv.append((is_q, later_i & earlier_j & (base == (jj & ~(2 * s - 1)))))
            s //= 2
        stacks.append(jnp.concatenate(mats, axis=0))
        levels.append(lv)

    lbs = [lb_ref[d] for d in range(2)]
    log_lb = [jnp.log(x) for x in lbs]
    log_1m = [jnp.log(1.0 - x) for x in lbs]
    hsl = [slice(h * LANE, (h + 1) * LANE) for h in range(HG_HEADS)]

    def group(gi, carry):
        chunks = [gi * HG_UNROLL + u for u in range(HG_UNROLL)]
        rows = [pl.ds(pl.multiple_of(ci * c, c), c) for ci in chunks]
        erows = [pl.ds(pl.multiple_of(ci * SUBLANE, SUBLANE), SUBLANE) for ci in chunks]
        q = [q_ref[r, :].astype(F32) * HG_DK ** -0.5 for r in rows]
        ud = [(u, d) for u in range(HG_UNROLL) for d in range(2)]
        lf, kk = [], []
        for u, d in ud:
            fr = f_refs[d][rows[u], :].astype(F32)
            ls = jnp.minimum(fr, 0.0) - jnp.log(1.0 + jnp.exp(-jnp.abs(fr)))
            bb = log_1m[d] + ls
            lf.append(jnp.maximum(log_lb[d], bb) + jnp.log(1.0 + jnp.exp(-jnp.abs(log_lb[d] - bb))))
            kk.append((1.0 - lbs[d]) / (1.0 + jnp.exp(fr)))
        parts = [_split3(x) for x in lf]
        cs = [[_dot(stacks[d], pt) for pt in parts[j]] for j, (u, d) in enumerate(ud)]
        cs = [a + b + cc_ for a, b, cc_ in cs]
        xs, masks = [], []
        for j, (u, d) in enumerate(ud):
            bcum, btot = cs[j][0:c], cs[j][c:2 * c]
            oi_ref, qd_ref, ke_ref, e_ref = outs[d]
            qd_ref[rows[u], :] = (q[u] * jnp.exp(bcum)).astype(BF16)
            ke_ref[rows[u], :] = (kk[j] * jnp.exp(btot - bcum)).astype(BF16)
            e_ref[erows[u], :] = jnp.exp(btot[0:SUBLANE, :])
            lv_x = []
            for li, (is_q, mask) in enumerate(levels[d]):
                r = cs[j][(2 + li) * c:(3 + li) * c]
                e = jnp.exp(jnp.where(is_q, bcum - r, r - bcum))
                lv_x.append((jnp.where(is_q, q[u], kk[j]) * e).astype(BF16))
            xs.append(lv_x)
        att = []
        for j, (u, d) in enumerate(ud):
            qb, kkb = q[u].astype(BF16), kk[j].astype(BF16)
            a = [jnp.where(eye, _dot_nt(qb[:, hs], kkb[:, hs]), 0.0) for hs in hsl]
            for li, (is_q, mask) in enumerate(levels[d]):
                for h, hs in enumerate(hsl):
                    xh = xs[j][li][:, hs]
                    a[h] = a[h] + jnp.where(mask, _dot_nt(xh, xh), 0.0)
            att.append([x.astype(BF16) for x in a])
        for j, (u, d) in enumerate(ud):
            vv = i_ref[rows[u], :]
            o = [_dot(att[j][h], vv[:, hs]) for h, hs in enumerate(hsl)]
            outs[d][0][rows[u], 0:LANE] = (o[0] + o[1]).astype(BF16)
            outs[d][0][rows[u], LANE:2 * LANE] = (o[2] + o[3]).astype(BF16)
        return carry

    lax.fori_loop(0, nchunks // HG_UNROLL, group, 0)


def _hgrn_chunk(z, lb, tb=SCAN_BLOCK):
    t = z.shape[0]
    nch = tb // HG_C
    ow = HG_HEADS * HG_DV
    zspec = lambda off: pl.BlockSpec((tb, HW), lambda i: (i, off // HW))
    wide = pl.BlockSpec((tb, HW), lambda i: (i, 0))
    narrow = pl.BlockSpec((tb, ow), lambda i: (i, 0))
    espec = pl.BlockSpec((nch * SUBLANE, HW), lambda i: (i, 0))
    ws, ns = jax.ShapeDtypeStruct((t, HW), BF16), jax.ShapeDtypeStruct((t, ow), BF16)
    es = jax.ShapeDtypeStruct((t // HG_C * SUBLANE, HW), F32)
    return pl.pallas_call(
        functools.partial(_hgrn_chunk_kernel, nchunks=nch), name="hgrn_chunk",
        grid=(t // tb,),
        in_specs=[zspec(Z_HQ), zspec(Z_HFF), zspec(Z_HFB), zspec(Z_HI),
                  pl.BlockSpec((2, 1, HW), lambda i: (0, 0, 0))],
        out_specs=[narrow, narrow, wide, wide, wide, wide, espec, espec],
        out_shape=[ns, ns, ws, ws, ws, ws, es, es],
        compiler_params=_cp(("parallel",)),
    )(z, z, z, z, lb.reshape(2, 1, HW))


def _hgrn_state_kernel(oif_ref, oib_ref, qdf_ref, qdb_ref, kef_ref, keb_ref, ef_ref, eb_ref, vf_ref, vb_ref,
                       of_ref, ob_ref, s_ref, *, nchunks):
    c = HG_C

    @pl.when(pl.program_id(1) == 0)
    def _():
        s_ref[...] = jnp.zeros_like(s_ref)

    dirs = ((oif_ref, qdf_ref, kef_ref, ef_ref, vf_ref, of_ref), (oib_ref, qdb_ref, keb_ref, eb_ref, vb_ref, ob_ref))
    state = [[s_ref[d, h] for h in range(HG_HEADS)] for d in range(2)]
    hsl = [slice(h * LANE, (h + 1) * LANE) for h in range(HG_HEADS)]
    chunk_of = lambda ci, d: ci if d == 0 else nchunks - 1 - ci
    rows_of = lambda ci, d: slice(chunk_of(ci, d) * c, (chunk_of(ci, d) + 1) * c)
    inc = [[[_dot_tn(dirs[d][4][rows_of(ci, d), hs], dirs[d][2][rows_of(ci, d), hs]) for hs in hsl]
            for d in range(2)] for ci in range(nchunks)]
    for ci in range(nchunks):
        inter = [[_dot_nt(dirs[d][1][rows_of(ci, d), hs], state[d][h].astype(BF16)) for h, hs in enumerate(hsl)]
                 for d in range(2)]
        for d, (oi_ref, qd_ref, ke_ref, e_ref, v_ref, o_ref) in enumerate(dirs):
            rows, cc = rows_of(ci, d), chunk_of(ci, d)
            for h, hs in enumerate(hsl):
                state[d][h] = state[d][h] * e_ref[cc * SUBLANE:cc * SUBLANE + 1, hs] + inc[ci][d][h]
            oi = oi_ref[rows, :].astype(F32)
            o_ref[rows, 0:LANE] = (oi[:, 0:LANE] + inter[d][0] + inter[d][1]).astype(BF16)
            o_ref[rows, LANE:2 * LANE] = (oi[:, LANE:2 * LANE] + inter[d][2] + inter[d][3]).astype(BF16)
    for d in range(2):
        for h in range(HG_HEADS):
            s_ref[d, h] = state[d][h]


def _hgrn_state(z, oi0, oi1, qd0, qd1, ke0, ke1, e0, e1, batch, seq, ctx_len, tb=SCAN_BLOCK):
    t = z.shape[0]
    nlb, ncb = seq // tb, ctx_len // tb
    nch = tb // HG_C
    ow = HG_HEADS * HG_DV
    blk = functools.partial(_scan_block, n_lat_blocks=nlb, n_ctx_blocks=ncb, batch=batch)
    fwd = lambda b, j: (blk(b, 0, j), 0)
    bwd = lambda b, j: (blk(b, 1, j), 0)
    both = lambda shape: [pl.BlockSpec(shape, fwd), pl.BlockSpec(shape, bwd)]
    vcol = Z_HI // HW
    os_ = jax.ShapeDtypeStruct((t, ow), BF16)
    return pl.pallas_call(
        functools.partial(_hgrn_state_kernel, nchunks=nch), name="hgrn_state",
        grid=(batch, nlb + ncb),
        in_specs=(both((tb, ow)) + both((tb, HW)) + both((tb, HW)) + both((nch * SUBLANE, HW))
                  + [pl.BlockSpec((tb, HW), lambda b, j: (blk(b, 0, j), vcol)),
                     pl.BlockSpec((tb, HW), lambda b, j: (blk(b, 1, j), vcol))]),
        out_specs=both((tb, ow)),
        out_shape=[os_, os_],
        scratch_shapes=[pltpu.VMEM((2, HG_HEADS, LANE, HG_DK), F32)],
        compiler_params=_cp(("parallel", "arbitrary")),
    )(oi0, oi1, qd0, qd1, ke0, ke1, e0, e1, z, z)


def _mla_prep_kernel(cq_ref, ckv_ref, kr_ref, krp_ref, cos_ref, sin_ref, gq_ref, gkv_ref,
                     wqa_ref, wqb_ref, wk_ref, wv_ref, q_ref, k_ref, v_ref):
    cos = cos_ref[...]
    sin = sin_ref[...]
    cqn = _rms(cq_ref[...].astype(F32), gq_ref[...]).astype(BF16)
    qa = _dot(cqn, wqa_ref[...])
    qb = _dot(cqn, wqb_ref[...])
    scale = (MLA_NOPE + MLA_ROPE) ** -0.5
    ckvn = _rms(ckv_ref[...].astype(F32), gkv_ref[...]).astype(BF16)
    kn = _dot(ckvn, wk_ref[...])
    kr = kr_ref[...].astype(F32) * cos + krp_ref[...].astype(F32) * sin
    for h in range(MLA_HEADS):
        hs = slice(h * LANE, (h + 1) * LANE)
        q_ref[:, hs] = ((qa[:, hs] * cos + qb[:, hs] * sin) * scale).astype(BF16)
        k_ref[:, hs] = (kn[:, hs] + kr).astype(BF16)
    v_ref[...] = _dot(ckvn, wv_ref[...]).astype(BF16)


def _mla_prep(z, cos_t, sin_t, gq, gkv, wqa, wqb, wk, wv, n_lat_rows, seq, tm=256):
    t = z.shape[0]
    lat_tiles, seq_tiles = n_lat_rows // tm, seq // tm
    hw = MLA_HEADS * LANE

    def tab(i):
        return (jnp.where(i < lat_tiles, i % seq_tiles, seq_tiles), 0)

    full = lambda a: pl.BlockSpec(a.shape, lambda i: (0, 0))
    return pl.pallas_call(
        _mla_prep_kernel, name="mla_prep",
        grid=(t // tm,),
        in_specs=[pl.BlockSpec((tm, MLA_Q_RANK), lambda i: (i, Z_CQ // MLA_Q_RANK)),
                  pl.BlockSpec((tm, LANE), lambda i: (i, Z_CKV // LANE)),
                  pl.BlockSpec((tm, LANE), lambda i: (i, Z_KR // LANE)),
                  pl.BlockSpec((tm, LANE), lambda i: (i, Z_KRP // LANE)),
                  pl.BlockSpec((tm, LANE), tab),
                  pl.BlockSpec((tm, LANE), tab),
                  full(gq), full(gkv), full(wqa), full(wqb), full(wk), full(wv)],
        out_specs=[pl.BlockSpec((tm, hw), lambda i: (i, 0))] * 3,
        out_shape=[jax.ShapeDtypeStruct((t, hw), BF16)] * 3,
        compiler_params=_cp(("parallel",)),
    )(z, z, z, z, cos_t, sin_t, gq, gkv, wqa, wqb, wk, wv)


def _attn_kernel(q_ref, kl_ref, vl_ref, kc_ref, vc_ref, o_ref, *, lat_tiles):
    i = pl.program_id(1)

    def run(use_latent):
        halves = []
        for h in range(MLA_HEADS):
            hs = slice(h * LANE, (h + 1) * LANE)
            q = q_ref[:, hs]
            sc = _dot_nt(q, kc_ref[:, hs])
            m = jnp.max(sc, axis=-1, keepdims=True)
            if use_latent:
                sl = _dot_nt(q, kl_ref[:, hs])
                m = jnp.maximum(m, jnp.max(sl, axis=-1, keepdims=True))
            pc = jnp.exp(sc - m)
            den = jnp.sum(pc, axis=-1, keepdims=True)
            acc = _dot(pc.astype(BF16), vc_ref[:, hs])
            if use_latent:
                pl_ = jnp.exp(sl - m)
                den = den + jnp.sum(pl_, axis=-1, keepdims=True)
                acc = acc + _dot(pl_.astype(BF16), vl_ref[:, hs])
            halves.append(acc / den)
        o_ref[:, 0:LANE] = (halves[0] + halves[1]).astype(BF16)
        o_ref[:, LANE:2 * LANE] = (halves[2] + halves[3]).astype(BF16)

    @pl.when(i < lat_tiles)
    def _():
        run(True)

    @pl.when(i >= lat_tiles)
    def _():
        run(False)


def _attention(q, k, v, batch, seq, ctx_len, with_ctx, tq=256):
    t = q.shape[0]
    hw = MLA_HEADS * LANE
    lat_tiles = seq // tq
    ctx_q_tiles = ctx_len // tq if with_ctx else 0
    ctx_base_q = batch * lat_tiles
    ctx_base_k = batch * seq // ctx_len

    def qmap(b, i):
        return (jnp.where(i < lat_tiles, b * lat_tiles + i,
                          ctx_base_q + b * (ctx_len // tq) + (i - lat_tiles)), 0)

    return pl.pallas_call(
        functools.partial(_attn_kernel, lat_tiles=lat_tiles), name="mla_attn",
        grid=(batch, lat_tiles + ctx_q_tiles),
        in_specs=[pl.BlockSpec((tq, hw), qmap),
                  pl.BlockSpec((seq, hw), lambda b, i: (b, 0)),
                  pl.BlockSpec((seq, hw), lambda b, i: (b, 0)),
                  pl.BlockSpec((ctx_len, hw), lambda b, i: (ctx_base_k + b, 0)),
                  pl.BlockSpec((ctx_len, hw), lambda b, i: (ctx_base_k + b, 0))],
        out_specs=pl.BlockSpec((tq, MLA_HEADS * MLA_DV), qmap),
        out_shape=jax.ShapeDtypeStruct((t, MLA_HEADS * MLA_DV), BF16),
        compiler_params=_cp(("parallel", "arbitrary")),
    )(q, k, v, k, v)


def _out_proj_kernel(h_ref, mod_ref, og0_ref, og1_ref, gg_ref, ym_ref, oh0_ref, oh1_ref, hg_ref,
                     gn_ref, hn_ref, wo_ref, gpost_ref, o_ref):
    tm = h_ref.shape[0]
    og = og0_ref[...].astype(F32) + og1_ref[...].astype(F32)
    gate = gg_ref[...].astype(F32)
    acc = jnp.zeros((tm, D_MODEL), F32)
    for h in range(GDN_HEADS):
        hs = slice(h * LANE, (h + 1) * LANE)
        g = gate[:, hs]
        y = _rms(og[:, hs], gn_ref[...]) * (g * _sigmoid(g))
        acc = acc + _dot(y.astype(BF16), wo_ref[h * LANE:(h + 1) * LANE, :])
    base = GDN_HEADS * GDN_DV
    acc = acc + _dot(ym_ref[...], wo_ref[base:base + MLA_HEADS * MLA_DV, :])
    base += MLA_HEADS * MLA_DV
    oh = oh0_ref[...].astype(F32) + oh1_ref[...].astype(F32)
    hgate = hg_ref[...].astype(F32)
    lane = lax.broadcasted_iota(jnp.int32, (1, LANE), 1)
    lo = lane < HG_DV
    for p in range(HG_HEADS // 2):
        ps = slice(p * LANE, (p + 1) * LANE)
        x = oh[:, ps]
        sq = x * x
        ms_lo = jnp.sum(jnp.where(lo, sq, 0.0), axis=-1, keepdims=True) / HG_DV
        ms_hi = jnp.sum(jnp.where(lo, 0.0, sq), axis=-1, keepdims=True) / HG_DV
        ms = jnp.where(lo, ms_lo, ms_hi)
        g = hgate[:, ps]
        y = x * lax.rsqrt(ms + EPS) * hn_ref[...] * (g * _sigmoid(g))
        acc = acc + _dot(y.astype(BF16), wo_ref[base + p * LANE:base + (p + 1) * LANE, :])
    m = mod_ref[0]
    o_ref[...] = h_ref[...] + m[:, 2 * D_MODEL:3 * D_MODEL] * _rms(acc, gpost_ref[...])


def _out_proj(h, mods, og0, og1, z, ym, oh0, oh1, gdn_norm, hgrn_norm, wo, gpost, n_rows, n_lat_rows, seq, tm=512):
    midx = lambda i: (_mod_index(i, tm, n_lat_rows, seq), 0, 0)
    ow = HG_HEADS * HG_DV
    full = lambda a: pl.BlockSpec(a.shape, lambda i: (0, 0))
    row = lambda w: pl.BlockSpec((tm, w), lambda i: (i, 0))
    hn2 = jnp.concatenate([hgrn_norm, hgrn_norm]).reshape(1, LANE)
    gn = gdn_norm.reshape(1, LANE)
    return pl.pallas_call(
        _out_proj_kernel, name="out_proj",
        grid=(n_rows // tm,),
        in_specs=[row(D_MODEL),
                  pl.BlockSpec((1, 1, 6 * D_MODEL), midx),
                  row(GW), row(GW),
                  pl.BlockSpec((tm, GW), lambda i: (i, Z_GG // GW)),
                  row(MLA_HEADS * MLA_DV),
                  row(ow), row(ow),
                  pl.BlockSpec((tm, ow), lambda i: (i, Z_HGATE // ow)),
                  full(gn), full(hn2), full(wo),
                  pl.BlockSpec((1, D_MODEL), lambda i: (0, 0))],
        out_specs=row(D_MODEL),
        out_shape=jax.ShapeDtypeStruct((n_rows, D_MODEL), F32),
        compiler_params=_cp(("parallel",)),
    )(h, mods, og0, og1, z, ym, oh0, oh1, z, gn, hn2, wo, gpost.reshape(1, D_MODEL))


def _mlp_kernel(h_ref, mod_ref, gpre_ref, gpost_ref, w1_ref, w2_ref, o_ref, hm_ref, acc_ref):
    f = pl.program_id(1)

    @pl.when(f == 0)
    def _():
        m = mod_ref[0]
        y = _rms(h_ref[...], gpre_ref[...])
        hm_ref[...] = (y * (1.0 + m[:, 4 * D_MODEL:5 * D_MODEL]) + m[:, 3 * D_MODEL:4 * D_MODEL]).astype(BF16)
        acc_ref[...] = jnp.zeros_like(acc_ref)

    a = jnp.maximum(_dot(hm_ref[...], w1_ref[...]), 0.0)
    acc_ref[...] += _dot((a * a).astype(BF16), w2_ref[...])

    @pl.when(f == pl.num_programs(1) - 1)
    def _():
        m = mod_ref[0]
        o_ref[...] = h_ref[...] + m[:, 5 * D_MODEL:6 * D_MODEL] * _rms(acc_ref[...], gpost_ref[...])


def _mlp(h, mods, gpre, gpost, w1, w2, n_rows, n_lat_rows, seq, tm=1024, tf=1024):
    midx = lambda i, f: (_mod_index(i, tm, n_lat_rows, seq), 0, 0)
    return pl.pallas_call(
        _mlp_kernel, name="mlp",
        grid=(n_rows // tm, D_FF // tf),
        in_specs=[pl.BlockSpec((tm, D_MODEL), lambda i, f: (i, 0)),
                  pl.BlockSpec((1, 1, 6 * D_MODEL), midx),
                  pl.BlockSpec((1, D_MODEL), lambda i, f: (0, 0)),
                  pl.BlockSpec((1, D_MODEL), lambda i, f: (0, 0)),
                  pl.BlockSpec((D_MODEL, tf), lambda i, f: (0, f)),
                  pl.BlockSpec((tf, D_MODEL), lambda i, f: (f, 0))],
        out_specs=pl.BlockSpec((tm, D_MODEL), lambda i, f: (i, 0)),
        out_shape=jax.ShapeDtypeStruct((n_rows, D_MODEL), F32),
        scratch_shapes=[pltpu.VMEM((tm, D_MODEL), BF16), pltpu.VMEM((tm, D_MODEL), F32)],
        compiler_params=_cp(("parallel", "arbitrary")),
    )(h, mods, gpre.reshape(1, D_MODEL), gpost.reshape(1, D_MODEL), w1, w2)


def _arrange_w_in(w):
    bounds, tot = [], 0
    for sz in IN_SIZES[:-1]:
        tot += sz
        bounds.append(tot)
    (gq, gk, gv, gg, gb, ga, cq, ckv, kr, hq, hff, hfb, hi, hgate) = jnp.split(w, bounds, axis=1)
    zc = lambda n: jnp.zeros((w.shape[0], n), w.dtype)
    ba = jnp.concatenate([gb[:, 0:4], ga[:, 0:4], gb[:, 4:8], ga[:, 4:8], zc(LANE - 16)], axis=1)
    half = MLA_ROPE // 2
    krb = jnp.concatenate([zc(MLA_NOPE), kr, zc(LANE - MLA_NOPE - MLA_ROPE)], axis=1)
    krp = jnp.concatenate([zc(MLA_NOPE), -kr[:, half:], kr[:, :half], zc(LANE - MLA_NOPE - MLA_ROPE)], axis=1)
    his = []
    for h in range(HG_HEADS):
        blk = hi[:, h * HG_DV:(h + 1) * HG_DV]
        his += [blk, zc(HG_DV)] if h % 2 == 0 else [zc(HG_DV), blk]
    out = jnp.concatenate([gq, gk, gv, gg, hq, hff, hfb] + his + [hgate, cq, ba, ckv, krb, krp], axis=1)
    return out.astype(BF16)


def _arrange_mla(w_uq, w_ukv):
    half = MLA_ROPE // 2
    dq = MLA_NOPE + MLA_ROPE
    zq = lambda n: jnp.zeros((w_uq.shape[0], n), w_uq.dtype)
    zk = lambda n: jnp.zeros((w_ukv.shape[0], n), w_ukv.dtype)
    qa, qb, wk, wv = [], [], [], []
    for h in range(MLA_HEADS):
        nope = w_uq[:, h * dq:h * dq + MLA_NOPE]
        rope = w_uq[:, h * dq + MLA_NOPE:(h + 1) * dq]
        qa += [nope, rope, zq(LANE - dq)]
        qb += [zq(MLA_NOPE), -rope[:, half:], rope[:, :half], zq(LANE - dq)]
        kv = w_ukv[:, h * (MLA_NOPE + MLA_DV):(h + 1) * (MLA_NOPE + MLA_DV)]
        wk += [kv[:, :MLA_NOPE], zk(LANE - MLA_NOPE)]
        wv += [kv[:, MLA_NOPE:], zk(MLA_DV)] if h % 2 == 0 else [zk(MLA_DV), kv[:, MLA_NOPE:]]
    cat = lambda xs: jnp.concatenate(xs, axis=1).astype(BF16)
    return cat(qa), cat(qb), cat(wk), cat(wv)


def _rope_tables(seq, tile):
    per_axis = MLA_ROPE // 2
    inv = ROPE_BASE ** (-jnp.arange(0, per_axis, 2, dtype=F32) / per_axis)
    rows = seq // GRID_W
    row = jnp.repeat(jnp.arange(rows, dtype=F32), GRID_W)
    col = jnp.tile(jnp.arange(GRID_W, dtype=F32), rows)
    ang = jnp.concatenate([row[:, None] * inv, col[:, None] * inv], axis=-1)
    cos, sin = jnp.cos(ang), jnp.sin(ang)
    one = jnp.ones((seq, MLA_NOPE), F32)
    pad1 = jnp.ones((seq, LANE - MLA_NOPE - MLA_ROPE), F32)
    cos_t = jnp.concatenate([one, cos, cos, pad1], axis=1)
    sin_t = jnp.concatenate([0 * one, sin, sin, 0 * pad1], axis=1)
    cos_t = jnp.concatenate([cos_t, jnp.ones((tile, LANE), F32)], axis=0)
    sin_t = jnp.concatenate([sin_t, jnp.zeros((tile, LANE), F32)], axis=0)
    return cos_t, sin_t


def kernel(x, c, ctx, c_ctx, w_ada, b_ada, norm_mix_pre, norm_mix_post, norm_mlp_pre, norm_mlp_post,
           w_in, w_out, gdn_conv, gdn_a_log, gdn_dt_bias, gdn_norm, mla_q_norm, mla_kv_norm,
           mla_w_uq, mla_w_ukv, hgrn_lb_logits, hgrn_norm, w_mlp1, w_mlp2):
    batch, seq, _ = x.shape
    ctx_len = ctx.shape[1]
    depth = w_in.shape[0]
    n_lat = batch * seq
    n_all = n_lat + batch * ctx_len
    assert seq % 1024 == 0 and ctx_len % SCAN_BLOCK == 0 and (batch * ctx_len) % 1024 == 0

    cos_t, sin_t = _rope_tables(seq, 256)
    lb_cum = jnp.cumsum(jax.nn.softmax(hgrn_lb_logits.astype(F32), axis=0), axis=0)
    lower_bounds = lb_cum - lb_cum[0]
    cvec = jnp.concatenate([c_ctx[None, :], c, jnp.zeros((16 - 1 - batch, D_MODEL), F32)], axis=0)

    h = jnp.concatenate([x.reshape(n_lat, D_MODEL), ctx.reshape(batch * ctx_len, D_MODEL)], axis=0)
    for l in range(depth):
        last = l == depth - 1
        n_rows = n_lat if last else n_all
        mods = _ada(cvec, w_ada[l], b_ada[l])[:1 + batch].reshape(1 + batch, 1, 6 * D_MODEL)
        z = _in_proj(h, mods, norm_mix_pre[l], _arrange_w_in(w_in[l]), n_lat, seq)
        qkv = _gdn_prep(z, gdn_conv[l], n_lat, seq, ctx_len)
        gx0, gx1, ge0, ge1 = _gdn_chunk(qkv, z, gdn_a_log[l], gdn_dt_bias[l])
        og0, og1 = _gdn_state(gx0, gx1, ge0, ge1, batch, seq, ctx_len)
        wqa, wqb, wk, wv = _arrange_mla(mla_w_uq[l], mla_w_ukv[l])
        q, k, v = _mla_prep(z, cos_t, sin_t, mla_q_norm[l].reshape(1, -1), mla_kv_norm[l].reshape(1, -1),
                            wqa, wqb, wk, wv, n_lat, seq)
        ym = _attention(q, k, v, batch, seq, ctx_len, with_ctx=not last)
        oi0, oi1, qd0, qd1, ke0, ke1, he0, he1 = _hgrn_chunk(z, lower_bounds[l])
        oh0, oh1 = _hgrn_state(z, oi0, oi1, qd0, qd1, ke0, ke1, he0, he1, batch, seq, ctx_len)
        h = _out_proj(h, mods, og0, og1, z, ym, oh0, oh1, gdn_norm[l], hgrn_norm[l], w_out[l].astype(BF16),
                      norm_mix_post[l], n_rows, n_lat, seq)
        h = _mlp(h, mods, norm_mlp_pre[l], norm_mlp_post[l], w_mlp1[l].astype(BF16), w_mlp2[l].astype(BF16),
                 n_rows, n_lat, seq)
    return h[:n_lat].reshape(batch, seq, D_MODEL)
```

```python
import functools

import jax
import jax.numpy as jnp
from jax import lax
from jax.experimental import pallas as pl
from jax.experimental.pallas import tpu as pltpu

F32 = jnp.float32
BF16 = jnp.bfloat16

D_MODEL = 1024
D_FF = 4 * D_MODEL
GRID_W = 64
EPS = 1e-6
ROPE_BASE = 10000.0
GDN_HEADS, GDN_DK, GDN_DV, CONV_W = 4, 128, 128, 5
MLA_HEADS, MLA_Q_RANK, MLA_KV_RANK, MLA_NOPE, MLA_ROPE, MLA_DV = 4, 256, 128, 64, 32, 64
HG_HEADS, HG_DK, HG_DV = 4, 128, 64
IN_SIZES = (512, 512, 512, 512, 8, 8, 256, 128, 32, 512, 512, 512, 256, 256)

LANE = 128
SUBLANE = 8
VMEM_LIMIT = 56 * 1024 * 1024

Z_QKV, Z_GG, Z_HQ, Z_HFF, Z_HFB, Z_HI, Z_HGATE, Z_CQ, Z_BA, Z_CKV, Z_KR, Z_KRP, NZ = (
    0, 1536, 2048, 2560, 3072, 3584, 4096, 4352, 4608, 4736, 4864, 4992, 5120)

SCAN_BLOCK = 256
GDN_C = 64
GDN_SUB = 16
GDN_UNROLL = 2
HG_C = 64
HG_UNROLL = 2
GW = GDN_HEADS * GDN_DV
HW = HG_HEADS * HG_DK


def _cp(sem, vmem=VMEM_LIMIT):
    return pltpu.CompilerParams(dimension_semantics=sem, vmem_limit_bytes=vmem)


def _dot(a, b):
    return jnp.dot(a, b, preferred_element_type=F32)


def _dot_nt(a, b):
    return lax.dot_general(a, b, (((1,), (1,)), ((), ())), preferred_element_type=F32)


def _dot_tn(a, b):
    return lax.dot_general(a, b, (((0,), (0,)), ((), ())), preferred_element_type=F32)


def _split3(x):
    hi = x.astype(BF16)
    r = x - hi.astype(F32)
    mid = r.astype(BF16)
    lo = (r - mid.astype(F32)).astype(BF16)
    return hi, mid, lo


def _dot01(m01, x):
    hi, mid, lo = _split3(x)
    return _dot(m01, hi) + _dot(m01, mid) + _dot(m01, lo)


def _dot01_tn(x, m01):
    hi, mid, lo = _split3(x)
    return _dot_tn(hi, m01) + _dot_tn(mid, m01) + _dot_tn(lo, m01)


def _sigmoid(x):
    return 1.0 / (1.0 + jnp.exp(-x))


def _softplus(x):
    return jnp.maximum(x, 0.0) + jnp.log(1.0 + jnp.exp(-jnp.abs(x)))


def _rms(x, g):
    ms = jnp.mean(x * x, axis=-1, keepdims=True)
    return x * lax.rsqrt(ms + EPS) * g


def _b01(mask):
    return jnp.where(mask, 1.0, 0.0).astype(BF16)


LOG2E = 1.4426950408889634


def _den_lane(h):
    return LANE - 1 if h % 2 == 0 else 0


def _ada_kernel(c_ref, w_ref, b_ref, o_ref):
    c = c_ref[...]
    s = c * _sigmoid(c)
    o_ref[...] = jnp.dot(s, w_ref[...], preferred_element_type=F32,
                         precision=lax.Precision.HIGHEST) + b_ref[...]


def _ada(cvec, w, b):
    n = w.shape[1]
    tn = 1024
    return pl.pallas_call(
        _ada_kernel, name="ada_mod",
        grid=(n // tn,),
        in_specs=[pl.BlockSpec(cvec.shape, lambda j: (0, 0)),
                  pl.BlockSpec((w.shape[0], tn), lambda j: (0, j)),
                  pl.BlockSpec((1, tn), lambda j: (0, j))],
        out_specs=pl.BlockSpec((cvec.shape[0], tn), lambda j: (0, j)),
        out_shape=jax.ShapeDtypeStruct((cvec.shape[0], n), F32),
        compiler_params=_cp(("parallel",)),
    )(cvec, w, b.reshape(1, n))


def _in_proj_kernel(h_ref, mod_ref, g_ref, w_ref, z_ref, hm_ref, *, tn):
    m = mod_ref[0]
    y = _rms(h_ref[...], g_ref[...])
    hm_ref[...] = (y * (1.0 + m[:, D_MODEL:2 * D_MODEL]) + m[:, 0:D_MODEL]).astype(BF16)
    for n in range(NZ // tn):
        z_ref[:, n * tn:(n + 1) * tn] = _dot(hm_ref[...], w_ref[:, n * tn:(n + 1) * tn]).astype(BF16)


def _mod_index(i, tm, n_lat_rows, seq):
    lat_tiles = n_lat_rows // tm
    return jnp.where(i < lat_tiles, 1 + i // (seq // tm), 0)


def _in_proj(h, mods, g, w, n_lat_rows, seq, tm=512, tn=512):
    t = h.shape[0]
    return pl.pallas_call(
        functools.partial(_in_proj_kernel, tn=tn), name="in_proj",
        grid=(t // tm,),
        in_specs=[pl.BlockSpec((tm, D_MODEL), lambda i: (i, 0)),
                  pl.BlockSpec((1, 1, 6 * D_MODEL), lambda i: (_mod_index(i, tm, n_lat_rows, seq), 0, 0)),
                  pl.BlockSpec((1, D_MODEL), lambda i: (0, 0)),
                  pl.BlockSpec((D_MODEL, NZ), lambda i: (0, 0))],
        out_specs=pl.BlockSpec((tm, NZ), lambda i: (i, 0)),
        out_shape=jax.ShapeDtypeStruct((t, NZ), BF16),
        scratch_shapes=[pltpu.VMEM((tm, D_MODEL), BF16)],
        compiler_params=_cp(("parallel",)),
    )(h, mods, g.reshape(1, D_MODEL), w)


HALO = 16


def _gdn_prep_kernel(zc_ref, zp_ref, zn_ref, cw_ref, o_ref, buf_ref, *, tm, lat_tiles, seq_tiles, ctx_tiles):
    i = pl.program_id(0)
    r = jnp.where(i < lat_tiles, i % seq_tiles, (i - lat_tiles) % ctx_tiles)
    n = jnp.where(i < lat_tiles, seq_tiles, ctx_tiles)
    pm = jnp.where(r == 0, 0.0, 1.0)
    nm = jnp.where(r == n - 1, 0.0, 1.0)
    buf_ref[0:HALO, :] = zp_ref[...].astype(F32) * pm
    buf_ref[HALO:HALO + tm, :] = zc_ref[...].astype(F32)
    buf_ref[HALO + tm:2 * HALO + tm, :] = zn_ref[...].astype(F32) * nm
    half = CONV_W // 2
    for cb in range(3 * GDN_HEADS):
        cs = slice(cb * LANE, (cb + 1) * LANE)
        acc = jnp.zeros((tm, LANE), F32)
        for j in range(CONV_W):
            acc = acc + buf_ref[HALO - half + j:HALO - half + j + tm, cs] * cw_ref[j:j + 1, cs]
        y = acc * _sigmoid(acc)
        if cb < 2 * GDN_HEADS:
            y = y * lax.rsqrt(jnp.sum(y * y, axis=-1, keepdims=True) + EPS)
            if cb < GDN_HEADS:
                y = y * GDN_DK ** -0.5
        o_ref[:, cs] = y.astype(BF16)


def _gdn_prep(z, conv_w, n_lat_rows, seq, ctx_len, tm=256):
    t = z.shape[0]
    w = 3 * GDN_HEADS * LANE
    hb = tm // HALO
    last = t // HALO - 1
    kern = functools.partial(_gdn_prep_kernel, tm=tm, lat_tiles=n_lat_rows // tm,
                             seq_tiles=seq // tm, ctx_tiles=ctx_len // tm)
    return pl.pallas_call(
        kern, name="gdn_prep",
        grid=(t // tm,),
        in_specs=[pl.BlockSpec((tm, w), lambda i: (i, 0)),
                  pl.BlockSpec((HALO, w), lambda i: (jnp.maximum(i * hb - 1, 0), 0)),
                  pl.BlockSpec((HALO, w), lambda i: (jnp.minimum((i + 1) * hb, last), 0)),
                  pl.BlockSpec((CONV_W, w), lambda i: (0, 0))],
        out_specs=pl.BlockSpec((tm, w), lambda i: (i, 0)),
        out_shape=jax.ShapeDtypeStruct((t, w), BF16),
        scratch_shapes=[pltpu.VMEM((tm + 2 * HALO, w), F32)],
        compiler_params=_cp(("parallel",)),
    )(z, z, z, conv_w)


def _scan_block(b, d, j, n_lat_blocks, n_ctx_blocks, batch):
    jc = j if d == 0 else n_ctx_blocks - 1 - j
    jl = j - n_ctx_blocks if d == 0 else n_lat_blocks - 1 - (j - n_ctx_blocks)
    return jnp.where(j < n_ctx_blocks, batch * n_lat_blocks + b * n_ctx_blocks + jc, b * n_lat_blocks + jl)


def _iotas(c):
    return lax.broadcasted_iota(jnp.int32, (c, c), 0), lax.broadcasted_iota(jnp.int32, (c, c), 1)


def _gdn_chunk_kernel(qkv_ref, ba_ref, alog_ref, dtb_ref, x0_ref, x1_ref, e0_ref, e1_ref, *, nchunks):
    c = GDN_C
    ii, jj = _iotas(c)
    eye = ii == jj
    blk = (ii // GDN_SUB) == (jj // GDN_SUB)
    low, up = ii >= jj, ii <= jj
    incl = (low, up)
    strict = (ii > jj, ii < jj)
    lm_all = jnp.concatenate([_b01(low), _b01(up), jnp.ones((c, c), BF16)], axis=0)
    um_all = jnp.concatenate([_b01(up), _b01(low)], axis=1)
    neg_a = -jnp.exp(alog_ref[...])
    dtb = dtb_ref[...]
    x_refs, e_refs = (x0_ref, x1_ref), (e0_ref, e1_ref)

    ident = jnp.where(eye, 1.0, 0.0)

    def group(gi, carry):
        chunks = [gi * GDN_UNROLL + u for u in range(GDN_UNROLL)]
        rows = [pl.ds(pl.multiple_of(ci * c, c), c) for ci in chunks]
        erows = [pl.ds(pl.multiple_of(ci * SUBLANE, SUBLANE), SUBLANE) for ci in chunks]
        ba = [ba_ref[r, :].astype(F32) for r in rows]
        la_all = [neg_a * _softplus(x + dtb) for x in ba]
        beta_all = [_sigmoid(x) for x in ba]
        parts = [_split3(x) for x in la_all]
        g3 = [[_dot(lm_all, pt) for pt in p3] for p3 in parts]
        gt = [[_dot_tn(pt, um_all) for pt in p3] for p3 in parts]
        g3 = [a + b + cc_ for a, b, cc_ in g3]
        gt = [a + b + cc_ for a, b, cc_ in gt]
        uh = [(u, h) for u in range(GDN_UNROLL) for h in range(GDN_HEADS)]
        q = [qkv_ref[rows[u], h * LANE:(h + 1) * LANE] for u, h in uh]
        k = [qkv_ref[rows[u], (GDN_HEADS + h) * LANE:(GDN_HEADS + h + 1) * LANE] for u, h in uh]
        v = [qkv_ref[rows[u], (2 * GDN_HEADS + h) * LANE:(2 * GDN_HEADS + h + 1) * LANE] for u, h in uh]
        gram = [_dot_nt(jnp.concatenate([k[i], q[i]], axis=0), k[i]) for i in range(len(uh))]
        ch = [(i, u, h, d) for i, (u, h) in enumerate(uh) for d in range(2)]
        beta, gcol, tot, q_att, dg, off, eg = [], [], [], [], [], [], []
        for i, u, h, d in ch:
            lb_, la_ = h + 8 * d, 4 + h + 8 * d
            beta.append(beta_all[u][:, lb_:lb_ + 1])
            gcol.append(g3[u][d * c:(d + 1) * c, la_:la_ + 1])
            tot.append(g3[u][2 * c:3 * c, la_:la_ + 1])
            grow = gt[u][la_:la_ + 1, d * c:(d + 1) * c]
            dec = jnp.exp(jnp.where(incl[d], gcol[-1] - grow, -1e30))
            nmat = jnp.where(strict[d], (beta[-1] * gram[i][0:c]) * dec, 0.0)
            q_att.append((gram[i][c:2 * c] * dec).astype(BF16))
            dg.append(jnp.where(blk, nmat, 0.0).astype(BF16))
            off.append(jnp.where(blk, 0.0, nmat).astype(BF16))
            eg.append(jnp.exp(gcol[-1]))
        n = len(ch)
        t = [ident - dg[j].astype(F32) for j in range(n)]
        p = [_dot(dg[j], dg[j]).astype(BF16) for j in range(n)]
        for lvl in range(3):
            t = [t[j] + _dot(t[j].astype(BF16), p[j]) for j in range(n)]
            if lvl < 2:
                p = [_dot(p[j], p[j]).astype(BF16) for j in range(n)]
        tb = [x.astype(BF16) for x in t]
        m = [_dot(tb[j], off[j]).astype(BF16) for j in range(n)]
        rhs = []
        for j, (i, u, h, d) in enumerate(ch):
            kb = k[i].astype(F32) * beta[j]
            rhs.append(jnp.concatenate([v[i].astype(F32) * beta[j], kb * eg[j]], axis=1).astype(BF16))
        x = [_dot(tb[j], rhs[j]) for j in range(n)]
        m2 = [_dot(m[j], m[j]).astype(BF16) for j in range(n)]
        x = [x[j] - _dot(m[j], x[j].astype(BF16)) for j in range(n)]
        xb = [(x[j] + _dot(m2[j], x[j].astype(BF16))).astype(BF16) for j in range(n)]
        qa = [_dot(q_att[j], xb[j]) for j in range(n)]
        for j, (i, u, h, d) in enumerate(ch):
            xr, r = x_refs[d], rows[u]
            xr[r, 0 * GW + h * LANE:0 * GW + (h + 1) * LANE] = xb[j][:, 0:LANE]
            xr[r, 1 * GW + h * LANE:1 * GW + (h + 1) * LANE] = xb[j][:, LANE:2 * LANE]
            xr[r, 2 * GW + h * LANE:2 * GW + (h + 1) * LANE] = (
                q[i].astype(F32) * eg[j] - qa[j][:, LANE:2 * LANE]).astype(BF16)
            xr[r, 3 * GW + h * LANE:3 * GW + (h + 1) * LANE] = (
                k[i].astype(F32) * jnp.exp(tot[j] - gcol[j])).astype(BF16)
            xr[r, 4 * GW + h * LANE:4 * GW + (h + 1) * LANE] = qa[j][:, 0:LANE].astype(BF16)
            e_refs[d][erows[u], h * LANE:(h + 1) * LANE] = jnp.broadcast_to(
                jnp.exp(tot[j][0:SUBLANE, :]), (SUBLANE, LANE))
        return carry

    lax.fori_loop(0, nchunks // GDN_UNROLL, group, 0)


def _gdn_chunk(qkv, z, a_log, dt_bias, tb=SCAN_BLOCK):
    t = qkv.shape[0]
    w = 3 * GDN_HEADS * LANE
    nch = tb // GDN_C

    def lanes(p):
        v = jnp.zeros((LANE,), F32)
        v = v.at[4:8].set(p[0].astype(F32)).at[12:16].set(p[1].astype(F32))
        return v.reshape(1, LANE)

    xs = jax.ShapeDtypeStruct((t, 5 * GW), BF16)
    es = jax.ShapeDtypeStruct((t // GDN_C * SUBLANE, GW), F32)
    xspec = pl.BlockSpec((tb, 5 * GW), lambda i: (i, 0))
    espec = pl.BlockSpec((nch * SUBLANE, GW), lambda i: (i, 0))
    return pl.pallas_call(
        functools.partial(_gdn_chunk_kernel, nchunks=nch), name="gdn_chunk",
        grid=(t // tb,),
        in_specs=[pl.BlockSpec((tb, w), lambda i: (i, 0)),
                  pl.BlockSpec((tb, LANE), lambda i: (i, Z_BA // LANE)),
                  pl.BlockSpec((1, LANE), lambda i: (0, 0)),
                  pl.BlockSpec((1, LANE), lambda i: (0, 0))],
        out_specs=[xspec, xspec, espec, espec],
        out_shape=[xs, xs, es, es],
        compiler_params=_cp(("parallel",)),
    )(qkv, z, lanes(a_log), lanes(dt_bias))


def _gdn_state_kernel(xf_ref, xb_ref, ef_ref, eb_ref, of_ref, ob_ref, s_ref, *, nchunks):
    c = GDN_C

    @pl.when(pl.program_id(1) == 0)
    def _():
        s_ref[...] = jnp.zeros_like(s_ref)

    dirs = ((xf_ref, ef_ref, of_ref), (xb_ref, eb_ref, ob_ref))
    state = [[s_ref[d, h] for h in range(GDN_HEADS)] for d in range(2)]
    dh = [(d, h) for d in range(2) for h in range(GDN_HEADS)]
    col = lambda g, h: slice(g * GW + h * LANE, g * GW + (h + 1) * LANE)
    for ci in range(nchunks):
        cc = [ci, nchunks - 1 - ci]
        rows = [slice(cc[d] * c, (cc[d] + 1) * c) for d in range(2)]
        ws = [_dot(jnp.concatenate([dirs[d][0][rows[d], col(1, h)], dirs[d][0][rows[d], col(2, h)]], axis=0),
                   state[d][h].astype(BF16)) for d, h in dh]
        v_new = [(dirs[d][0][rows[d], col(0, h)].astype(F32) - ws[i][0:c]).astype(BF16) for i, (d, h) in enumerate(dh)]
        upd = [_dot_tn(dirs[d][0][rows[d], col(3, h)], v_new[i]) for i, (d, h) in enumerate(dh)]
        for i, (d, h) in enumerate(dh):
            x_ref, e_ref, o_ref = dirs[d]
            o_ref[rows[d], h * LANE:(h + 1) * LANE] = (
                x_ref[rows[d], col(4, h)].astype(F32) + ws[i][c:2 * c]).astype(BF16)
            decay = e_ref[cc[d] * SUBLANE:cc[d] * SUBLANE + 1, h * LANE:(h + 1) * LANE]
            state[d][h] = state[d][h] * decay + upd[i]
    for d in range(2):
        for h in range(GDN_HEADS):
            s_ref[d, h] = state[d][h]


def _gdn_state(x0, x1, e0, e1, batch, seq, ctx_len, tb=SCAN_BLOCK):
    t = x0.shape[0]
    nlb, ncb = seq // tb, ctx_len // tb
    nch = tb // GDN_C
    blk = functools.partial(_scan_block, n_lat_blocks=nlb, n_ctx_blocks=ncb, batch=batch)
    fwd = lambda b, j: (blk(b, 0, j), 0)
    bwd = lambda b, j: (blk(b, 1, j), 0)
    os_ = jax.ShapeDtypeStruct((t, GW), BF16)
    return pl.pallas_call(
        functools.partial(_gdn_state_kernel, nchunks=nch), name="gdn_state",
        grid=(batch, nlb + ncb),
        in_specs=[pl.BlockSpec((tb, 5 * GW), fwd), pl.BlockSpec((tb, 5 * GW), bwd),
                  pl.BlockSpec((nch * SUBLANE, GW), fwd), pl.BlockSpec((nch * SUBLANE, GW), bwd)],
        out_specs=[pl.BlockSpec((tb, GW), fwd), pl.BlockSpec((tb, GW), bwd)],
        out_shape=[os_, os_],
        scratch_shapes=[pltpu.VMEM((2, GDN_HEADS, GDN_DK, GDN_DV), F32)],
        compiler_params=_cp(("parallel", "arbitrary")),
    )(x0, x1, e0, e1)


def _hgrn_chunk_kernel(q_ref, ff_ref, fb_ref, i_ref, lb_ref, oi0_ref, oi1_ref, qd0_ref, qd1_ref,
                       ke0_ref, ke1_ref, e0_ref, e1_ref, *, nchunks):
    c = HG_C
    ii, jj = _iotas(c)
    ri = lax.broadcasted_iota(jnp.int32, (c, 1), 0)
    eye = ii == jj
    f_refs = (ff_ref, fb_ref)
    outs = ((oi0_ref, qd0_ref, ke0_ref, e0_ref), (oi1_ref, qd1_ref, ke1_ref, e1_ref))

    stacks, levels = [], []
    for d in range(2):
        incl = (ii >= jj) if d == 0 else (ii <= jj)
        stacks.append(jnp.concatenate([_b01(incl), jnp.ones((c, c), BF16)], axis=0))
        lv = []
        s = c // 2
        while s >= 1:
            base = ii & ~(2 * s - 1)
            upper_i = (ii & (2 * s - 1)) >= s
            upper_j = (jj & (2 * s - 1)) >= s
            later_i = upper_i if d == 0 else ~upper_i
            earlier_j = ~upper_j if d == 0 else upper_j
            upper_r = (ri & (2 * s - 1)) >= s
            is_q = upper_r if d == 0 else ~upper_r
            lv.append((s, is_q, later_i & earlier_j & (base == (jj & ~(2 * s - 1)))))
            s //= 2
        levels.append(lv)

    def boundary_rows(b, s, d):
        ref = s - 1 + d
        if 2 * s >= SUBLANE:
            return jnp.concatenate([jnp.broadcast_to(b[base + ref:base + ref + 1, :], (2 * s, b.shape[1]))
                                    for base in range(0, c, 2 * s)], axis=0)
        pos = ri & (2 * s - 1)
        out = b
        for o in range(2 * s):
            if o != ref:
                out = jnp.where(pos == o, pltpu.roll(b, (o - ref) % c, axis=0), out)
        return out

    lbs = [lb_ref[d] for d in range(2)]
    log_lb = [jnp.log(x) for x in lbs]
    log_1m = [jnp.log(1.0 - x) for x in lbs]
    hsl = [slice(h * LANE, (h + 1) * LANE) for h in range(HG_HEADS)]

    def group(gi, carry):
        chunks = [gi * HG_UNROLL + u for u in range(HG_UNROLL)]
        rows = [pl.ds(pl.multiple_of(ci * c, c), c) for ci in chunks]
        erows = [pl.ds(pl.multiple_of(ci * SUBLANE, SUBLANE), SUBLANE) for ci in chunks]
        q = [q_ref[r, :].astype(F32) * HG_DK ** -0.5 for r in rows]
        ud = [(u, d) for u in range(HG_UNROLL) for d in range(2)]
        lf, kk = [], []
        for u, d in ud:
            fr = f_refs[d][rows[u], :].astype(F32)
            ls = jnp.minimum(fr, 0.0) - jnp.log(1.0 + jnp.exp(-jnp.abs(fr)))
            bb = log_1m[d] + ls
            lf.append(jnp.maximum(log_lb[d], bb) + jnp.log(1.0 + jnp.exp(-jnp.abs(log_lb[d] - bb))))
            kk.append((1.0 - lbs[d]) / (1.0 + jnp.exp(fr)))
        parts = [_split3(x) for x in lf]
        cs = [[_dot(stacks[d], pt) for pt in parts[j]] for j, (u, d) in enumerate(ud)]
        cs = [a + b + cc_ for a, b, cc_ in cs]
        xs, masks = [], []
        for j, (u, d) in enumerate(ud):
            bcum, btot = cs[j][0:c], cs[j][c:2 * c]
            oi_ref, qd_ref, ke_ref, e_ref = outs[d]
            qd_ref[rows[u], :] = (q[u] * jnp.exp(bcum)).astype(BF16)
            ke_ref[rows[u], :] = (kk[j] * jnp.exp(btot - bcum)).astype(BF16)
            e_ref[erows[u], :] = jnp.exp(btot[0:SUBLANE, :])
            lv_x = []
            for s, is_q, mask in levels[d]:
                r = boundary_rows(bcum, s, d)
                e = jnp.exp(jnp.where(is_q, bcum - r, r - bcum))
                lv_x.append((jnp.where(is_q, q[u], kk[j]) * e).astype(BF16))
            xs.append(lv_x)
        att = []
        for j, (u, d) in enumerate(ud):
            qb, kkb = q[u].astype(BF16), kk[j].astype(BF16)
            a = [jnp.where(eye, _dot_nt(qb[:, hs], kkb[:, hs]), 0.0) for hs in hsl]
            for li, (s, is_q, mask) in enumerate(levels[d]):
                for h, hs in enumerate(hsl):
                    xh = xs[j][li][:, hs]
                    a[h] = a[h] + jnp.where(mask, _dot_nt(xh, xh), 0.0)
            att.append([x.astype(BF16) for x in a])
        for j, (u, d) in enumerate(ud):
            vv = i_ref[rows[u], :]
            o = [_dot(att[j][h], vv[:, hs]) for h, hs in enumerate(hsl)]
            outs[d][0][rows[u], 0:LANE] = (o[0] + o[1]).astype(BF16)
            outs[d][0][rows[u], LANE:2 * LANE] = (o[2] + o[3]).astype(BF16)
        return carry

    lax.fori_loop(0, nchunks // HG_UNROLL, group, 0)


def _hgrn_chunk(z, lb, tb=SCAN_BLOCK):
    t = z.shape[0]
    nch = tb // HG_C
    ow = HG_HEADS * HG_DV
    zspec = lambda off: pl.BlockSpec((tb, HW), lambda i: (i, off // HW))
    wide = pl.BlockSpec((tb, HW), lambda i: (i, 0))
    narrow = pl.BlockSpec((tb, ow), lambda i: (i, 0))
    espec = pl.BlockSpec((nch * SUBLANE, HW), lambda i: (i, 0))
    ws, ns = jax.ShapeDtypeStruct((t, HW), BF16), jax.ShapeDtypeStruct((t, ow), BF16)
    es = jax.ShapeDtypeStruct((t // HG_C * SUBLANE, HW), F32)
    return pl.pallas_call(
        functools.partial(_hgrn_chunk_kernel, nchunks=nch), name="hgrn_chunk",
        grid=(t // tb,),
        in_specs=[zspec(Z_HQ), zspec(Z_HFF), zspec(Z_HFB), zspec(Z_HI),
                  pl.BlockSpec((2, 1, HW), lambda i: (0, 0, 0))],
        out_specs=[narrow, narrow, wide, wide, wide, wide, espec, espec],
        out_shape=[ns, ns, ws, ws, ws, ws, es, es],
        compiler_params=_cp(("parallel",)),
    )(z, z, z, z, lb.reshape(2, 1, HW))


def _hgrn_state_kernel(oif_ref, oib_ref, qdf_ref, qdb_ref, kef_ref, keb_ref, ef_ref, eb_ref, vf_ref, vb_ref,
                       of_ref, ob_ref, s_ref, *, nchunks):
    c = HG_C

    @pl.when(pl.program_id(1) == 0)
    def _():
        s_ref[...] = jnp.zeros_like(s_ref)

    dirs = ((oif_ref, qdf_ref, kef_ref, ef_ref, vf_ref, of_ref), (oib_ref, qdb_ref, keb_ref, eb_ref, vb_ref, ob_ref))
    state = [[s_ref[d, h] for h in range(HG_HEADS)] for d in range(2)]
    hsl = [slice(h * LANE, (h + 1) * LANE) for h in range(HG_HEADS)]
    chunk_of = lambda ci, d: ci if d == 0 else nchunks - 1 - ci
    rows_of = lambda ci, d: slice(chunk_of(ci, d) * c, (chunk_of(ci, d) + 1) * c)
    inc = [[[_dot_tn(dirs[d][4][rows_of(ci, d), hs], dirs[d][2][rows_of(ci, d), hs]) for hs in hsl]
            for d in range(2)] for ci in range(nchunks)]
    for ci in range(nchunks):
        inter = [[_dot_nt(dirs[d][1][rows_of(ci, d), hs], state[d][h].astype(BF16)) for h, hs in enumerate(hsl)]
                 for d in range(2)]
        for d, (oi_ref, qd_ref, ke_ref, e_ref, v_ref, o_ref) in enumerate(dirs):
            rows, cc = rows_of(ci, d), chunk_of(ci, d)
            for h, hs in enumerate(hsl):
                state[d][h] = state[d][h] * e_ref[cc * SUBLANE:cc * SUBLANE + 1, hs] + inc[ci][d][h]
            oi = oi_ref[rows, :].astype(F32)
            o_ref[rows, 0:LANE] = (oi[:, 0:LANE] + inter[d][0] + inter[d][1]).astype(BF16)
            o_ref[rows, LANE:2 * LANE] = (oi[:, LANE:2 * LANE] + inter[d][2] + inter[d][3]).astype(BF16)
    for d in range(2):
        for h in range(HG_HEADS):
            s_ref[d, h] = state[d][h]


def _hgrn_state(z, oi0, oi1, qd0, qd1, ke0, ke1, e0, e1, batch, seq, ctx_len, tb=SCAN_BLOCK):
    t = z.shape[0]
    nlb, ncb = seq // tb, ctx_len // tb
    nch = tb // HG_C
    ow = HG_HEADS * HG_DV
    blk = functools.partial(_scan_block, n_lat_blocks=nlb, n_ctx_blocks=ncb, batch=batch)
    fwd = lambda b, j: (blk(b, 0, j), 0)
    bwd = lambda b, j: (blk(b, 1, j), 0)
    both = lambda shape: [pl.BlockSpec(shape, fwd), pl.BlockSpec(shape, bwd)]
    vcol = Z_HI // HW
    os_ = jax.ShapeDtypeStruct((t, ow), BF16)
    return pl.pallas_call(
        functools.partial(_hgrn_state_kernel, nchunks=nch), name="hgrn_state",
        grid=(batch, nlb + ncb),
        in_specs=(both((tb, ow)) + both((tb, HW)) + both((tb, HW)) + both((nch * SUBLANE, HW))
                  + [pl.BlockSpec((tb, HW), lambda b, j: (blk(b, 0, j), vcol)),
                     pl.BlockSpec((tb, HW), lambda b, j: (blk(b, 1, j), vcol))]),
        out_specs=both((tb, ow)),
        out_shape=[os_, os_],
        scratch_shapes=[pltpu.VMEM((2, HG_HEADS, LANE, HG_DK), F32)],
        compiler_params=_cp(("parallel", "arbitrary")),
    )(oi0, oi1, qd0, qd1, ke0, ke1, e0, e1, z, z)


def _mla_prep_kernel(cq_ref, ckv_ref, kr_ref, krp_ref, cos_ref, sin_ref, gq_ref, gkv_ref,
                     wqa_ref, wqb_ref, wk_ref, wv_ref, q_ref, k_ref, v_ref):
    cos = cos_ref[...]
    sin = sin_ref[...]
    cqn = _rms(cq_ref[...].astype(F32), gq_ref[...]).astype(BF16)
    qa = _dot(cqn, wqa_ref[...])
    qb = _dot(cqn, wqb_ref[...])
    scale = (MLA_NOPE + MLA_ROPE) ** -0.5 * LOG2E
    ckvn = _rms(ckv_ref[...].astype(F32), gkv_ref[...]).astype(BF16)
    kn = _dot(ckvn, wk_ref[...])
    kr = kr_ref[...].astype(F32) * cos + krp_ref[...].astype(F32) * sin
    for h in range(MLA_HEADS):
        hs = slice(h * LANE, (h + 1) * LANE)
        q_ref[:, hs] = ((qa[:, hs] * cos + qb[:, hs] * sin) * scale).astype(BF16)
        k_ref[:, hs] = (kn[:, hs] + kr).astype(BF16)
    lane = lax.broadcasted_iota(jnp.int32, (1, MLA_HEADS * LANE), 1)
    ones_lane = functools.reduce(jnp.logical_or, [lane == h * LANE + _den_lane(h) for h in range(MLA_HEADS)])
    v_ref[...] = (_dot(ckvn, wv_ref[...]) + jnp.where(ones_lane, 1.0, 0.0)).astype(BF16)


def _mla_prep(z, cos_t, sin_t, gq, gkv, wqa, wqb, wk, wv, n_lat_rows, seq, tm=256):
    t = z.shape[0]
    lat_tiles, seq_tiles = n_lat_rows // tm, seq // tm
    hw = MLA_HEADS * LANE

    def tab(i):
        return (jnp.where(i < lat_tiles, i % seq_tiles, seq_tiles), 0)

    full = lambda a: pl.BlockSpec(a.shape, lambda i: (0, 0))
    return pl.pallas_call(
        _mla_prep_kernel, name="mla_prep",
        grid=(t // tm,),
        in_specs=[pl.BlockSpec((tm, MLA_Q_RANK), lambda i: (i, Z_CQ // MLA_Q_RANK)),
                  pl.BlockSpec((tm, LANE), lambda i: (i, Z_CKV // LANE)),
                  pl.BlockSpec((tm, LANE), lambda i: (i, Z_KR // LANE)),
                  pl.BlockSpec((tm, LANE), lambda i: (i, Z_KRP // LANE)),
                  pl.BlockSpec((tm, LANE), tab),
                  pl.BlockSpec((tm, LANE), tab),
                  full(gq), full(gkv), full(wqa), full(wqb), full(wk), full(wv)],
        out_specs=[pl.BlockSpec((tm, hw), lambda i: (i, 0))] * 3,
        out_shape=[jax.ShapeDtypeStruct((t, hw), BF16)] * 3,
        compiler_params=_cp(("parallel",)),
    )(z, z, z, z, cos_t, sin_t, gq, gkv, wqa, wqb, wk, wv)


def _attn_kernel(q_ref, kl_ref, vl_ref, kc_ref, vc_ref, o_ref, *, lat_tiles):
    i = pl.program_id(1)

    lane = lax.broadcasted_iota(jnp.int32, (1, LANE), 1)

    def run(use_latent):
        halves = []
        for h in range(MLA_HEADS):
            hs = slice(h * LANE, (h + 1) * LANE)
            q = q_ref[:, hs]
            sc = _dot_nt(q, kc_ref[:, hs])
            m = jnp.max(sc, axis=-1, keepdims=True)
            if use_latent:
                sl = _dot_nt(q, kl_ref[:, hs])
                m = jnp.maximum(m, jnp.max(sl, axis=-1, keepdims=True))
            acc = _dot(jnp.exp2((sc - m).astype(BF16)), vc_ref[:, hs])
            if use_latent:
                acc = acc + _dot(jnp.exp2((sl - m).astype(BF16)), vl_ref[:, hs])
            dl = _den_lane(h)
            den = acc[:, dl:dl + 1]
            halves.append(jnp.where(lane == dl, 0.0, acc / den))
        o_ref[:, 0:LANE] = (halves[0] + halves[1]).astype(BF16)
        o_ref[:, LANE:2 * LANE] = (halves[2] + halves[3]).astype(BF16)

    @pl.when(i < lat_tiles)
    def _():
        run(True)

    @pl.when(i >= lat_tiles)
    def _():
        run(False)


def _attention(q, k, v, batch, seq, ctx_len, with_ctx, tq=256):
    t = q.shape[0]
    hw = MLA_HEADS * LANE
    lat_tiles = seq // tq
    ctx_q_tiles = ctx_len // tq if with_ctx else 0
    ctx_base_q = batch * lat_tiles
    ctx_base_k = batch * seq // ctx_len

    def qmap(b, i):
        return (jnp.where(i < lat_tiles, b * lat_tiles + i,
                          ctx_base_q + b * (ctx_len // tq) + (i - lat_tiles)), 0)

    return pl.pallas_call(
        functools.partial(_attn_kernel, lat_tiles=lat_tiles), name="mla_attn",
        grid=(batch, lat_tiles + ctx_q_tiles),
        in_specs=[pl.BlockSpec((tq, hw), qmap),
                  pl.BlockSpec((seq, hw), lambda b, i: (b, 0)),
                  pl.BlockSpec((seq, hw), lambda b, i: (b, 0)),
                  pl.BlockSpec((ctx_len, hw), lambda b, i: (ctx_base_k + b, 0)),
                  pl.BlockSpec((ctx_len, hw), lambda b, i: (ctx_base_k + b, 0))],
        out_specs=pl.BlockSpec((tq, MLA_HEADS * MLA_DV), qmap),
        out_shape=jax.ShapeDtypeStruct((t, MLA_HEADS * MLA_DV), BF16),
        compiler_params=_cp(("parallel", "arbitrary")),
    )(q, k, v, k, v)


def _out_proj_kernel(h_ref, mod_ref, og0_ref, og1_ref, gg_ref, ym_ref, oh0_ref, oh1_ref, hg_ref,
                     gn_ref, hn_ref, wo_ref, gpost_ref, o_ref):
    tm = h_ref.shape[0]
    og = og0_ref[...].astype(F32) + og1_ref[...].astype(F32)
    gate = gg_ref[...].astype(F32)
    acc = jnp.zeros((tm, D_MODEL), F32)
    for h in range(GDN_HEADS):
        hs = slice(h * LANE, (h + 1) * LANE)
        g = gate[:, hs]
        y = _rms(og[:, hs], gn_ref[...]) * (g * _sigmoid(g))
        acc = acc + _dot(y.astype(BF16), wo_ref[h * LANE:(h + 1) * LANE, :])
    base = GDN_HEADS * GDN_DV
    acc = acc + _dot(ym_ref[...], wo_ref[base:base + MLA_HEADS * MLA_DV, :])
    base += MLA_HEADS * MLA_DV
    oh = oh0_ref[...].astype(F32) + oh1_ref[...].astype(F32)
    hgate = hg_ref[...].astype(F32)
    lane = lax.broadcasted_iota(jnp.int32, (1, LANE), 1)
    lo = lane < HG_DV
    for p in range(HG_HEADS // 2):
        ps = slice(p * LANE, (p + 1) * LANE)
        x = oh[:, ps]
        sq = x * x
        ms_lo = jnp.sum(jnp.where(lo, sq, 0.0), axis=-1, keepdims=True) / HG_DV
        ms_hi = jnp.sum(jnp.where(lo, 0.0, sq), axis=-1, keepdims=True) / HG_DV
        ms = jnp.where(lo, ms_lo, ms_hi)
        g = hgate[:, ps]
        y = x * lax.rsqrt(ms + EPS) * hn_ref[...] * (g * _sigmoid(g))
        acc = acc + _dot(y.astype(BF16), wo_ref[base + p * LANE:base + (p + 1) * LANE, :])
    m = mod_ref[0]
    o_ref[...] = h_ref[...] + m[:, 2 * D_MODEL:3 * D_MODEL] * _rms(acc, gpost_ref[...])


def _out_proj(h, mods, og0, og1, z, ym, oh0, oh1, gdn_norm, hgrn_norm, wo, gpost, n_rows, n_lat_rows, seq, tm=512):
    midx = lambda i: (_mod_index(i, tm, n_lat_rows, seq), 0, 0)
    ow = HG_HEADS * HG_DV
    full = lambda a: pl.BlockSpec(a.shape, lambda i: (0, 0))
    row = lambda w: pl.BlockSpec((tm, w), lambda i: (i, 0))
    hn2 = jnp.concatenate([hgrn_norm, hgrn_norm]).reshape(1, LANE)
    gn = gdn_norm.reshape(1, LANE)
    return pl.pallas_call(
        _out_proj_kernel, name="out_proj",
        grid=(n_rows // tm,),
        in_specs=[row(D_MODEL),
                  pl.BlockSpec((1, 1, 6 * D_MODEL), midx),
                  row(GW), row(GW),
                  pl.BlockSpec((tm, GW), lambda i: (i, Z_GG // GW)),
                  row(MLA_HEADS * MLA_DV),
                  row(ow), row(ow),
                  pl.BlockSpec((tm, ow), lambda i: (i, Z_HGATE // ow)),
                  full(gn), full(hn2), full(wo),
                  pl.BlockSpec((1, D_MODEL), lambda i: (0, 0))],
        out_specs=row(D_MODEL),
        out_shape=jax.ShapeDtypeStruct((n_rows, D_MODEL), F32),
        compiler_params=_cp(("parallel",)),
    )(h, mods, og0, og1, z, ym, oh0, oh1, z, gn, hn2, wo, gpost.reshape(1, D_MODEL))


def _mlp_kernel(h_ref, mod_ref, gpre_ref, gpost_ref, w1_ref, w2_ref, o_ref, hm_ref, acc_ref):
    f = pl.program_id(1)

    @pl.when(f == 0)
    def _():
        m = mod_ref[0]
        y = _rms(h_ref[...], gpre_ref[...])
        hm_ref[...] = (y * (1.0 + m[:, 4 * D_MODEL:5 * D_MODEL]) + m[:, 3 * D_MODEL:4 * D_MODEL]).astype(BF16)
        acc_ref[...] = jnp.zeros_like(acc_ref)

    a = jnp.maximum(_dot(hm_ref[...], w1_ref[...]), 0.0)
    acc_ref[...] += _dot((a * a).astype(BF16), w2_ref[...])

    @pl.when(f == pl.num_programs(1) - 1)
    def _():
        m = mod_ref[0]
        o_ref[...] = h_ref[...] + m[:, 5 * D_MODEL:6 * D_MODEL] * _rms(acc_ref[...], gpost_ref[...])


def _mlp(h, mods, gpre, gpost, w1, w2, n_rows, n_lat_rows, seq, tm=1024, tf=1024):
    midx = lambda i, f: (_mod_index(i, tm, n_lat_rows, seq), 0, 0)
    return pl.pallas_call(
        _mlp_kernel, name="mlp",
        grid=(n_rows // tm, D_FF // tf),
        in_specs=[pl.BlockSpec((tm, D_MODEL), lambda i, f: (i, 0)),
                  pl.BlockSpec((1, 1, 6 * D_MODEL), midx),
                  pl.BlockSpec((1, D_MODEL), lambda i, f: (0, 0)),
                  pl.BlockSpec((1, D_MODEL), lambda i, f: (0, 0)),
                  pl.BlockSpec((D_MODEL, tf), lambda i, f: (0, f)),
                  pl.BlockSpec((tf, D_MODEL), lambda i, f: (f, 0))],
        out_specs=pl.BlockSpec((tm, D_MODEL), lambda i, f: (i, 0)),
        out_shape=jax.ShapeDtypeStruct((n_rows, D_MODEL), F32),
        scratch_shapes=[pltpu.VMEM((tm, D_MODEL), BF16), pltpu.VMEM((tm, D_MODEL), F32)],
        compiler_params=_cp(("parallel", "arbitrary")),
    )(h, mods, gpre.reshape(1, D_MODEL), gpost.reshape(1, D_MODEL), w1, w2)


def _arrange_w_in(w):
    bounds, tot = [], 0
    for sz in IN_SIZES[:-1]:
        tot += sz
        bounds.append(tot)
    (gq, gk, gv, gg, gb, ga, cq, ckv, kr, hq, hff, hfb, hi, hgate) = jnp.split(w, bounds, axis=1)
    zc = lambda n: jnp.zeros((w.shape[0], n), w.dtype)
    ba = jnp.concatenate([gb[:, 0:4], ga[:, 0:4], gb[:, 4:8], ga[:, 4:8], zc(LANE - 16)], axis=1)
    half = MLA_ROPE // 2
    krb = jnp.concatenate([zc(MLA_NOPE), kr, zc(LANE - MLA_NOPE - MLA_ROPE)], axis=1)
    krp = jnp.concatenate([zc(MLA_NOPE), -kr[:, half:], kr[:, :half], zc(LANE - MLA_NOPE - MLA_ROPE)], axis=1)
    his = []
    for h in range(HG_HEADS):
        blk = hi[:, h * HG_DV:(h + 1) * HG_DV]
        his += [blk, zc(HG_DV)] if h % 2 == 0 else [zc(HG_DV), blk]
    out = jnp.concatenate([gq, gk, gv, gg, hq, hff, hfb] + his + [hgate, cq, ba, ckv, krb, krp], axis=1)
    return out.astype(BF16)


def _arrange_mla(w_uq, w_ukv):
    half = MLA_ROPE // 2
    dq = MLA_NOPE + MLA_ROPE
    zq = lambda n: jnp.zeros((w_uq.shape[0], n), w_uq.dtype)
    zk = lambda n: jnp.zeros((w_ukv.shape[0], n), w_ukv.dtype)
    qa, qb, wk, wv = [], [], [], []
    for h in range(MLA_HEADS):
        nope = w_uq[:, h * dq:h * dq + MLA_NOPE]
        rope = w_uq[:, h * dq + MLA_NOPE:(h + 1) * dq]
        qa += [nope, rope, zq(LANE - dq)]
        qb += [zq(MLA_NOPE), -rope[:, half:], rope[:, :half], zq(LANE - dq)]
        kv = w_ukv[:, h * (MLA_NOPE + MLA_DV):(h + 1) * (MLA_NOPE + MLA_DV)]
        wk += [kv[:, :MLA_NOPE], zk(LANE - MLA_NOPE)]
        wv += [kv[:, MLA_NOPE:], zk(MLA_DV)] if h % 2 == 0 else [zk(MLA_DV), kv[:, MLA_NOPE:]]
    cat = lambda xs: jnp.concatenate(xs, axis=1).astype(BF16)
    return cat(qa), cat(qb), cat(wk), cat(wv)


def _rope_tables(seq, tile):
    per_axis = MLA_ROPE // 2
    inv = ROPE_BASE ** (-jnp.arange(0, per_axis, 2, dtype=F32) / per_axis)
    rows = seq // GRID_W
    row = jnp.repeat(jnp.arange(rows, dtype=F32), GRID_W)
    col = jnp.tile(jnp.arange(GRID_W, dtype=F32), rows)
    ang = jnp.concatenate([row[:, None] * inv, col[:, None] * inv], axis=-1)
    cos, sin = jnp.cos(ang), jnp.sin(ang)
    one = jnp.ones((seq, MLA_NOPE), F32)
    pad1 = jnp.ones((seq, LANE - MLA_NOPE - MLA_ROPE), F32)
    cos_t = jnp.concatenate([one, cos, cos, pad1], axis=1)
    sin_t = jnp.concatenate([0 * one, sin, sin, 0 * pad1], axis=1)
    cos_t = jnp.concatenate([cos_t, jnp.ones((tile, LANE), F32)], axis=0)
    sin_t = jnp.concatenate([sin_t, jnp.zeros((tile, LANE), F32)], axis=0)
    return cos_t, sin_t


def kernel(x, c, ctx, c_ctx, w_ada, b_ada, norm_mix_pre, norm_mix_post, norm_mlp_pre, norm_mlp_post,
           w_in, w_out, gdn_conv, gdn_a_log, gdn_dt_bias, gdn_norm, mla_q_norm, mla_kv_norm,
           mla_w_uq, mla_w_ukv, hgrn_lb_logits, hgrn_norm, w_mlp1, w_mlp2):
    batch, seq, _ = x.shape
    ctx_len = ctx.shape[1]
    depth = w_in.shape[0]
    n_lat = batch * seq
    n_all = n_lat + batch * ctx_len
    assert seq % 1024 == 0 and ctx_len % SCAN_BLOCK == 0 and (batch * ctx_len) % 1024 == 0

    cos_t, sin_t = _rope_tables(seq, 256)
    lb_cum = jnp.cumsum(jax.nn.softmax(hgrn_lb_logits.astype(F32), axis=0), axis=0)
    lower_bounds = lb_cum - lb_cum[0]
    cvec = jnp.concatenate([c_ctx[None, :], c, jnp.zeros((16 - 1 - batch, D_MODEL), F32)], axis=0)

    h = jnp.concatenate([x.reshape(n_lat, D_MODEL), ctx.reshape(batch * ctx_len, D_MODEL)], axis=0)
    for l in range(depth):
        last = l == depth - 1
        n_rows = n_lat if last else n_all
        mods = _ada(cvec, w_ada[l], b_ada[l])[:1 + batch].reshape(1 + batch, 1, 6 * D_MODEL)
        z = _in_proj(h, mods, norm_mix_pre[l], _arrange_w_in(w_in[l]), n_lat, seq)
        qkv = _gdn_prep(z, gdn_conv[l], n_lat, seq, ctx_len)
        gx0, gx1, ge0, ge1 = _gdn_chunk(qkv, z, gdn_a_log[l], gdn_dt_bias[l])
        og0, og1 = _gdn_state(gx0, gx1, ge0, ge1, batch, seq, ctx_len)
        wqa, wqb, wk, wv = _arrange_mla(mla_w_uq[l], mla_w_ukv[l])
        q, k, v = _mla_prep(z, cos_t, sin_t, mla_q_norm[l].reshape(1, -1), mla_kv_norm[l].reshape(1, -1),
                            wqa, wqb, wk, wv, n_lat, seq)
        ym = _attention(q, k, v, batch, seq, ctx_len, with_ctx=not last)
        oi0, oi1, qd0, qd1, ke0, ke1, he0, he1 = _hgrn_chunk(z, lower_bounds[l])
        oh0, oh1 = _hgrn_state(z, oi0, oi1, qd0, qd1, ke0, ke1, he0, he1, batch, seq, ctx_len)
        h = _out_proj(h, mods, og0, og1, z, ym, oh0, oh1, gdn_norm[l], hgrn_norm[l], w_out[l].astype(BF16),
                      norm_mix_post[l], n_rows, n_lat, seq)
        h = _mlp(h, mods, norm_mlp_pre[l], norm_mlp_post[l], w_mlp1[l].astype(BF16), w_mlp2[l].astype(BF16),
                 n_rows, n_lat, seq)
    return h[:n_lat].reshape(batch, seq, D_MODEL)
```

```python
import functools

import jax
import jax.numpy as jnp
from jax import lax
from jax.experimental import pallas as pl
from jax.experimental.pallas import tpu as pltpu

F32 = jnp.float32
BF16 = jnp.bfloat16

D_MODEL = 1024
D_FF = 4 * D_MODEL
GRID_W = 64
EPS = 1e-6
ROPE_BASE = 10000.0
GDN_HEADS, GDN_DK, GDN_DV, CONV_W = 4, 128, 128, 5
MLA_HEADS, MLA_Q_RANK, MLA_KV_RANK, MLA_NOPE, MLA_ROPE, MLA_DV = 4, 256, 128, 64, 32, 64
HG_HEADS, HG_DK, HG_DV = 4, 128, 64
IN_SIZES = (512, 512, 512, 512, 8, 8, 256, 128, 32, 512, 512, 512, 256, 256)

LANE = 128
SUBLANE = 8
VMEM_LIMIT = 56 * 1024 * 1024

Z_QKV, Z_GG, Z_HQ, Z_HFF, Z_HFB, Z_HI, Z_HGATE, Z_CQ, Z_BA, Z_CKV, Z_KR, Z_KRP, NZ = (
    0, 1536, 2048, 2560, 3072, 3584, 4096, 4352, 4608, 4736, 4864, 4992, 5120)

SCAN_BLOCK = 256
PAR_BLOCK = 512
GDN_C = 64
GDN_SUB = 16
GDN_UNROLL = 4
HG_C = 64
HG_UNROLL = 2
GW = GDN_HEADS * GDN_DV
HW = HG_HEADS * HG_DK


def _cp(sem, vmem=VMEM_LIMIT):
    return pltpu.CompilerParams(dimension_semantics=sem, vmem_limit_bytes=vmem)


def _dot(a, b):
    return jnp.dot(a, b, preferred_element_type=F32)


def _dot_nt(a, b):
    return lax.dot_general(a, b, (((1,), (1,)), ((), ())), preferred_element_type=F32)


def _dot_tn(a, b):
    return lax.dot_general(a, b, (((0,), (0,)), ((), ())), preferred_element_type=F32)


def _split3(x):
    hi = x.astype(BF16)
    r = x - hi.astype(F32)
    mid = r.astype(BF16)
    lo = (r - mid.astype(F32)).astype(BF16)
    return hi, mid, lo


def _dot01(m01, x):
    hi, mid, lo = _split3(x)
    return _dot(m01, hi) + _dot(m01, mid) + _dot(m01, lo)


def _dot01_tn(x, m01):
    hi, mid, lo = _split3(x)
    return _dot_tn(hi, m01) + _dot_tn(mid, m01) + _dot_tn(lo, m01)


def _sigmoid(x):
    return 1.0 / (1.0 + jnp.exp(-x))


def _softplus(x):
    return jnp.maximum(x, 0.0) + jnp.log(1.0 + jnp.exp(-jnp.abs(x)))


def _rms(x, g):
    ms = jnp.mean(x * x, axis=-1, keepdims=True)
    return x * lax.rsqrt(ms + EPS) * g


def _b01(mask):
    return jnp.where(mask, 1.0, 0.0).astype(BF16)


LOG2E = 1.4426950408889634


def _den_lane(h):
    return LANE - 1 if h % 2 == 0 else 0


def _ada_kernel(c_ref, w_ref, b_ref, o_ref):
    c = c_ref[...]
    s = c * _sigmoid(c)
    o_ref[...] = jnp.dot(s, w_ref[...], preferred_element_type=F32,
                         precision=lax.Precision.HIGHEST) + b_ref[...]


def _ada(cvec, w, b):
    n = w.shape[1]
    tn = 1024
    return pl.pallas_call(
        _ada_kernel, name="ada_mod",
        grid=(n // tn,),
        in_specs=[pl.BlockSpec(cvec.shape, lambda j: (0, 0)),
                  pl.BlockSpec((w.shape[0], tn), lambda j: (0, j)),
                  pl.BlockSpec((1, tn), lambda j: (0, j))],
        out_specs=pl.BlockSpec((cvec.shape[0], tn), lambda j: (0, j)),
        out_shape=jax.ShapeDtypeStruct((cvec.shape[0], n), F32),
        compiler_params=_cp(("parallel",)),
    )(cvec, w, b.reshape(1, n))


def _in_proj_kernel(h_ref, mod_ref, g_ref, w_ref, z_ref, hm_ref, *, tn):
    m = mod_ref[0]
    y = _rms(h_ref[...], g_ref[...])
    hm_ref[...] = (y * (1.0 + m[:, D_MODEL:2 * D_MODEL]) + m[:, 0:D_MODEL]).astype(BF16)
    for n in range(NZ // tn):
        z_ref[:, n * tn:(n + 1) * tn] = _dot(hm_ref[...], w_ref[:, n * tn:(n + 1) * tn]).astype(BF16)


def _mod_index(i, tm, n_lat_rows, seq):
    lat_tiles = n_lat_rows // tm
    return jnp.where(i < lat_tiles, 1 + i // (seq // tm), 0)


def _in_proj(h, mods, g, w, n_lat_rows, seq, tm=512, tn=512):
    t = h.shape[0]
    return pl.pallas_call(
        functools.partial(_in_proj_kernel, tn=tn), name="in_proj",
        grid=(t // tm,),
        in_specs=[pl.BlockSpec((tm, D_MODEL), lambda i: (i, 0)),
                  pl.BlockSpec((1, 1, 6 * D_MODEL), lambda i: (_mod_index(i, tm, n_lat_rows, seq), 0, 0)),
                  pl.BlockSpec((1, D_MODEL), lambda i: (0, 0)),
                  pl.BlockSpec((D_MODEL, NZ), lambda i: (0, 0))],
        out_specs=pl.BlockSpec((tm, NZ), lambda i: (i, 0)),
        out_shape=jax.ShapeDtypeStruct((t, NZ), BF16),
        scratch_shapes=[pltpu.VMEM((tm, D_MODEL), BF16)],
        compiler_params=_cp(("parallel",)),
    )(h, mods, g.reshape(1, D_MODEL), w)


HALO = 16


def _gdn_prep_kernel(zc_ref, zp_ref, zn_ref, cw_ref, o_ref, buf_ref, *, tm, lat_tiles, seq_tiles, ctx_tiles):
    i = pl.program_id(0)
    r = jnp.where(i < lat_tiles, i % seq_tiles, (i - lat_tiles) % ctx_tiles)
    n = jnp.where(i < lat_tiles, seq_tiles, ctx_tiles)
    pm = jnp.where(r == 0, 0.0, 1.0)
    nm = jnp.where(r == n - 1, 0.0, 1.0)
    buf_ref[0:HALO, :] = zp_ref[...].astype(F32) * pm
    buf_ref[HALO:HALO + tm, :] = zc_ref[...].astype(F32)
    buf_ref[HALO + tm:2 * HALO + tm, :] = zn_ref[...].astype(F32) * nm
    half = CONV_W // 2
    for cb in range(3 * GDN_HEADS):
        cs = slice(cb * LANE, (cb + 1) * LANE)
        acc = jnp.zeros((tm, LANE), F32)
        for j in range(CONV_W):
            acc = acc + buf_ref[HALO - half + j:HALO - half + j + tm, cs] * cw_ref[j:j + 1, cs]
        y = acc * _sigmoid(acc)
        if cb < 2 * GDN_HEADS:
            y = y * lax.rsqrt(jnp.sum(y * y, axis=-1, keepdims=True) + EPS)
            if cb < GDN_HEADS:
                y = y * GDN_DK ** -0.5
        o_ref[:, cs] = y.astype(BF16)


def _gdn_prep(z, conv_w, n_lat_rows, seq, ctx_len, tm=256):
    t = z.shape[0]
    w = 3 * GDN_HEADS * LANE
    hb = tm // HALO
    last = t // HALO - 1
    kern = functools.partial(_gdn_prep_kernel, tm=tm, lat_tiles=n_lat_rows // tm,
                             seq_tiles=seq // tm, ctx_tiles=ctx_len // tm)
    return pl.pallas_call(
        kern, name="gdn_prep",
        grid=(t // tm,),
        in_specs=[pl.BlockSpec((tm, w), lambda i: (i, 0)),
                  pl.BlockSpec((HALO, w), lambda i: (jnp.maximum(i * hb - 1, 0), 0)),
                  pl.BlockSpec((HALO, w), lambda i: (jnp.minimum((i + 1) * hb, last), 0)),
                  pl.BlockSpec((CONV_W, w), lambda i: (0, 0))],
        out_specs=pl.BlockSpec((tm, w), lambda i: (i, 0)),
        out_shape=jax.ShapeDtypeStruct((t, w), BF16),
        scratch_shapes=[pltpu.VMEM((tm + 2 * HALO, w), F32)],
        compiler_params=_cp(("parallel",)),
    )(z, z, z, conv_w)


def _scan_block(b, d, j, n_lat_blocks, n_ctx_blocks, batch):
    jc = j if d == 0 else n_ctx_blocks - 1 - j
    jl = j - n_ctx_blocks if d == 0 else n_lat_blocks - 1 - (j - n_ctx_blocks)
    return jnp.where(j < n_ctx_blocks, batch * n_lat_blocks + b * n_ctx_blocks + jc, b * n_lat_blocks + jl)


def _iotas(c):
    return lax.broadcasted_iota(jnp.int32, (c, c), 0), lax.broadcasted_iota(jnp.int32, (c, c), 1)


def _gdn_chunk_kernel(qkv_ref, ba_ref, alog_ref, dtb_ref, x0_ref, x1_ref, e0_ref, e1_ref, *, nchunks):
    c = GDN_C
    ii, jj = _iotas(c)
    eye = ii == jj
    blk = (ii // GDN_SUB) == (jj // GDN_SUB)
    low, up = ii >= jj, ii <= jj
    incl = (low, up)
    strict = (ii > jj, ii < jj)
    lm_all = jnp.concatenate([_b01(low), _b01(up), jnp.ones((c, c), BF16)], axis=0)
    um_all = jnp.concatenate([_b01(up), _b01(low)], axis=1)
    neg_a = -jnp.exp(alog_ref[...])
    dtb = dtb_ref[...]
    x_refs, e_refs = (x0_ref, x1_ref), (e0_ref, e1_ref)

    ident = jnp.where(eye, 1.0, 0.0)

    def group(gi, carry):
        chunks = [gi * GDN_UNROLL + u for u in range(GDN_UNROLL)]
        rows = [pl.ds(pl.multiple_of(ci * c, c), c) for ci in chunks]
        erows = [pl.ds(pl.multiple_of(ci * SUBLANE, SUBLANE), SUBLANE) for ci in chunks]
        ba = [ba_ref[r, :].astype(F32) for r in rows]
        la_all = [neg_a * _softplus(x + dtb) for x in ba]
        beta_all = [_sigmoid(x) for x in ba]
        parts = [_split3(x) for x in la_all]
        g3 = [[_dot(lm_all, pt) for pt in p3] for p3 in parts]
        gt = [[_dot_tn(pt, um_all) for pt in p3] for p3 in parts]
        g3 = [a + b + cc_ for a, b, cc_ in g3]
        gt = [a + b + cc_ for a, b, cc_ in gt]
        uh = [(u, h) for u in range(GDN_UNROLL) for h in range(GDN_HEADS)]
        q = [qkv_ref[rows[u], h * LANE:(h + 1) * LANE] for u, h in uh]
        k = [qkv_ref[rows[u], (GDN_HEADS + h) * LANE:(GDN_HEADS + h + 1) * LANE] for u, h in uh]
        v = [qkv_ref[rows[u], (2 * GDN_HEADS + h) * LANE:(2 * GDN_HEADS + h + 1) * LANE] for u, h in uh]
        gram = [_dot_nt(jnp.concatenate([k[i], q[i]], axis=0), k[i]) for i in range(len(uh))]
        ch = [(i, u, h, d) for i, (u, h) in enumerate(uh) for d in range(2)]
        beta, gcol, tot, q_att, dg, off, eg = [], [], [], [], [], [], []
        for i, u, h, d in ch:
            lb_, la_ = h + 8 * d, 4 + h + 8 * d
            beta.append(beta_all[u][:, lb_:lb_ + 1])
            gcol.append(g3[u][d * c:(d + 1) * c, la_:la_ + 1])
            tot.append(g3[u][2 * c:3 * c, la_:la_ + 1])
            grow = gt[u][la_:la_ + 1, d * c:(d + 1) * c]
            dec = jnp.exp(jnp.where(incl[d], gcol[-1] - grow, -1e30))
            nmat = jnp.where(strict[d], (beta[-1] * gram[i][0:c]) * dec, 0.0)
            q_att.append((gram[i][c:2 * c] * dec).astype(BF16))
            dg.append(jnp.where(blk, nmat, 0.0).astype(BF16))
            off.append(jnp.where(blk, 0.0, nmat).astype(BF16))
            eg.append(jnp.exp(gcol[-1]))
        n = len(ch)
        t = [ident - dg[j].astype(F32) for j in range(n)]
        p = [_dot(dg[j], dg[j]).astype(BF16) for j in range(n)]
        for lvl in range(3):
            t = [t[j] + _dot(t[j].astype(BF16), p[j]) for j in range(n)]
            if lvl < 2:
                p = [_dot(p[j], p[j]).astype(BF16) for j in range(n)]
        tb = [x.astype(BF16) for x in t]
        m = [_dot(tb[j], off[j]).astype(BF16) for j in range(n)]
        rhs = []
        for j, (i, u, h, d) in enumerate(ch):
            kb = k[i].astype(F32) * beta[j]
            rhs.append(jnp.concatenate([v[i].astype(F32) * beta[j], kb * eg[j]], axis=1).astype(BF16))
        x = [_dot(tb[j], rhs[j]) for j in range(n)]
        m2 = [_dot(m[j], m[j]).astype(BF16) for j in range(n)]
        x = [x[j] - _dot(m[j], x[j].astype(BF16)) for j in range(n)]
        xb = [(x[j] + _dot(m2[j], x[j].astype(BF16))).astype(BF16) for j in range(n)]
        qa = [_dot(q_att[j], xb[j]) for j in range(n)]
        for j, (i, u, h, d) in enumerate(ch):
            xr, r = x_refs[d], rows[u]
            xr[r, 0 * GW + h * LANE:0 * GW + (h + 1) * LANE] = xb[j][:, 0:LANE]
            xr[r, 1 * GW + h * LANE:1 * GW + (h + 1) * LANE] = xb[j][:, LANE:2 * LANE]
            xr[r, 2 * GW + h * LANE:2 * GW + (h + 1) * LANE] = (
                q[i].astype(F32) * eg[j] - qa[j][:, LANE:2 * LANE]).astype(BF16)
            xr[r, 3 * GW + h * LANE:3 * GW + (h + 1) * LANE] = (
                k[i].astype(F32) * jnp.exp(tot[j] - gcol[j])).astype(BF16)
            xr[r, 4 * GW + h * LANE:4 * GW + (h + 1) * LANE] = qa[j][:, 0:LANE].astype(BF16)
            e_refs[d][erows[u], h * LANE:(h + 1) * LANE] = jnp.broadcast_to(
                jnp.exp(tot[j][0:SUBLANE, :]), (SUBLANE, LANE))
        return carry

    lax.fori_loop(0, nchunks // GDN_UNROLL, group, 0)


def _gdn_chunk(qkv, z, a_log, dt_bias, tb=PAR_BLOCK):
    t = qkv.shape[0]
    w = 3 * GDN_HEADS * LANE
    nch = tb // GDN_C

    def lanes(p):
        v = jnp.zeros((LANE,), F32)
        v = v.at[4:8].set(p[0].astype(F32)).at[12:16].set(p[1].astype(F32))
        return v.reshape(1, LANE)

    xs = jax.ShapeDtypeStruct((t, 5 * GW), BF16)
    es = jax.ShapeDtypeStruct((t // GDN_C * SUBLANE, GW), F32)
    xspec = pl.BlockSpec((tb, 5 * GW), lambda i: (i, 0))
    espec = pl.BlockSpec((nch * SUBLANE, GW), lambda i: (i, 0))
    return pl.pallas_call(
        functools.partial(_gdn_chunk_kernel, nchunks=nch), name="gdn_chunk",
        grid=(t // tb,),
        in_specs=[pl.BlockSpec((tb, w), lambda i: (i, 0)),
                  pl.BlockSpec((tb, LANE), lambda i: (i, Z_BA // LANE)),
                  pl.BlockSpec((1, LANE), lambda i: (0, 0)),
                  pl.BlockSpec((1, LANE), lambda i: (0, 0))],
        out_specs=[xspec, xspec, espec, espec],
        out_shape=[xs, xs, es, es],
        compiler_params=_cp(("parallel",)),
    )(qkv, z, lanes(a_log), lanes(dt_bias))


def _gdn_state_kernel(xf_ref, xb_ref, ef_ref, eb_ref, of_ref, ob_ref, s_ref, *, nchunks):
    c = GDN_C

    @pl.when(pl.program_id(1) == 0)
    def _():
        s_ref[...] = jnp.zeros_like(s_ref)

    dirs = ((xf_ref, ef_ref, of_ref), (xb_ref, eb_ref, ob_ref))
    state = [[s_ref[d, h] for h in range(GDN_HEADS)] for d in range(2)]
    dh = [(d, h) for d in range(2) for h in range(GDN_HEADS)]
    col = lambda g, h: slice(g * GW + h * LANE, g * GW + (h + 1) * LANE)
    for ci in range(nchunks):
        cc = [ci, nchunks - 1 - ci]
        rows = [slice(cc[d] * c, (cc[d] + 1) * c) for d in range(2)]
        ws = [_dot(jnp.concatenate([dirs[d][0][rows[d], col(1, h)], dirs[d][0][rows[d], col(2, h)]], axis=0),
                   state[d][h].astype(BF16)) for d, h in dh]
        v_new = [(dirs[d][0][rows[d], col(0, h)].astype(F32) - ws[i][0:c]).astype(BF16) for i, (d, h) in enumerate(dh)]
        upd = [_dot_tn(dirs[d][0][rows[d], col(3, h)], v_new[i]) for i, (d, h) in enumerate(dh)]
        for i, (d, h) in enumerate(dh):
            x_ref, e_ref, o_ref = dirs[d]
            o_ref[rows[d], h * LANE:(h + 1) * LANE] = (
                x_ref[rows[d], col(4, h)].astype(F32) + ws[i][c:2 * c]).astype(BF16)
            decay = e_ref[cc[d] * SUBLANE:cc[d] * SUBLANE + 1, h * LANE:(h + 1) * LANE]
            state[d][h] = state[d][h] * decay + upd[i]
    for d in range(2):
        for h in range(GDN_HEADS):
            s_ref[d, h] = state[d][h]


def _gdn_state(x0, x1, e0, e1, batch, seq, ctx_len, tb=SCAN_BLOCK):
    t = x0.shape[0]
    nlb, ncb = seq // tb, ctx_len // tb
    nch = tb // GDN_C
    blk = functools.partial(_scan_block, n_lat_blocks=nlb, n_ctx_blocks=ncb, batch=batch)
    fwd = lambda b, j: (blk(b, 0, j), 0)
    bwd = lambda b, j: (blk(b, 1, j), 0)
    os_ = jax.ShapeDtypeStruct((t, GW), BF16)
    return pl.pallas_call(
        functools.partial(_gdn_state_kernel, nchunks=nch), name="gdn_state",
        grid=(batch, nlb + ncb),
        in_specs=[pl.BlockSpec((tb, 5 * GW), fwd), pl.BlockSpec((tb, 5 * GW), bwd),
                  pl.BlockSpec((nch * SUBLANE, GW), fwd), pl.BlockSpec((nch * SUBLANE, GW), bwd)],
        out_specs=[pl.BlockSpec((tb, GW), fwd), pl.BlockSpec((tb, GW), bwd)],
        out_shape=[os_, os_],
        scratch_shapes=[pltpu.VMEM((2, GDN_HEADS, GDN_DK, GDN_DV), F32)],
        compiler_params=_cp(("parallel", "arbitrary")),
    )(x0, x1, e0, e1)


def _hgrn_chunk_kernel(q_ref, ff_ref, fb_ref, i_ref, lb_ref, oi0_ref, oi1_ref, qd0_ref, qd1_ref,
                       ke0_ref, ke1_ref, e0_ref, e1_ref, *, nchunks):
    c = HG_C
    ii, jj = _iotas(c)
    ri = lax.broadcasted_iota(jnp.int32, (c, 1), 0)
    eye = ii == jj
    f_refs = (ff_ref, fb_ref)
    outs = ((oi0_ref, qd0_ref, ke0_ref, e0_ref), (oi1_ref, qd1_ref, ke1_ref, e1_ref))

    stacks, levels = [], []
    for d in range(2):
        incl = (ii >= jj) if d == 0 else (ii <= jj)
        stacks.append(jnp.concatenate([_b01(incl), jnp.ones((c, c), BF16)], axis=0))
        lv = []
        s = c // 2
        while s >= 1:
            base = ii & ~(2 * s - 1)
            upper_i = (ii & (2 * s - 1)) >= s
            upper_j = (jj & (2 * s - 1)) >= s
            later_i = upper_i if d == 0 else ~upper_i
            earlier_j = ~upper_j if d == 0 else upper_j
            upper_r = (ri & (2 * s - 1)) >= s
            is_q = upper_r if d == 0 else ~upper_r
            lv.append((s, is_q, later_i & earlier_j & (base == (jj & ~(2 * s - 1)))))
            s //= 2
        levels.append(lv)

    def boundary_rows(b, s, d):
        ref = s - 1 + d
        if 2 * s >= SUBLANE:
            return jnp.concatenate([jnp.broadcast_to(b[base + ref:base + ref + 1, :], (2 * s, b.shape[1]))
                                    for base in range(0, c, 2 * s)], axis=0)
        pos = ri & (2 * s - 1)
        out = b
        for o in range(2 * s):
            if o != ref:
                out = jnp.where(pos == o, pltpu.roll(b, (o - ref) % c, axis=0), out)
        return out

    lbs = [lb_ref[d] for d in range(2)]
    log_lb = [jnp.log(x) for x in lbs]
    log_1m = [jnp.log(1.0 - x) for x in lbs]
    hsl = [slice(h * LANE, (h + 1) * LANE) for h in range(HG_HEADS)]

    def group(gi, carry):
        chunks = [gi * HG_UNROLL + u for u in range(HG_UNROLL)]
        rows = [pl.ds(pl.multiple_of(ci * c, c), c) for ci in chunks]
        erows = [pl.ds(pl.multiple_of(ci * SUBLANE, SUBLANE), SUBLANE) for ci in chunks]
        q = [q_ref[r, :].astype(F32) * HG_DK ** -0.5 for r in rows]
        ud = [(u, d) for u in range(HG_UNROLL) for d in range(2)]
        lf, kk = [], []
        for u, d in ud:
            fr = f_refs[d][rows[u], :].astype(F32)
            ls = jnp.minimum(fr, 0.0) - jnp.log(1.0 + jnp.exp(-jnp.abs(fr)))
            bb = log_1m[d] + ls
            lf.append((jnp.maximum(log_lb[d], bb) + jnp.log(1.0 + jnp.exp(-jnp.abs(log_lb[d] - bb)))) * LOG2E)
            kk.append((1.0 - lbs[d]) / (1.0 + jnp.exp(fr)))
        parts = [_split3(x) for x in lf]
        cs = [[_dot(stacks[d], pt) for pt in parts[j]] for j, (u, d) in enumerate(ud)]
        cs = [a + b + cc_ for a, b, cc_ in cs]
        xs, masks = [], []
        for j, (u, d) in enumerate(ud):
            bcum, btot = cs[j][0:c], cs[j][c:2 * c]
            oi_ref, qd_ref, ke_ref, e_ref = outs[d]
            qd_ref[rows[u], :] = (q[u] * jnp.exp2(bcum)).astype(BF16)
            ke_ref[rows[u], :] = (kk[j] * jnp.exp2(btot - bcum)).astype(BF16)
            e_ref[erows[u], :] = jnp.exp2(btot[0:SUBLANE, :])
            lv_x = []
            for s, is_q, mask in levels[d]:
                r = boundary_rows(bcum, s, d)
                e = jnp.exp2((bcum - r) * jnp.where(is_q, 1.0, -1.0))
                lv_x.append((jnp.where(is_q, q[u], kk[j]) * e).astype(BF16))
            xs.append(lv_x)
        att = []
        for j, (u, d) in enumerate(ud):
            qb, kkb = q[u].astype(BF16), kk[j].astype(BF16)
            a = [jnp.where(eye, _dot_nt(qb[:, hs], kkb[:, hs]), 0.0) for hs in hsl]
            for li, (s, is_q, mask) in enumerate(levels[d]):
                for h, hs in enumerate(hsl):
                    xh = xs[j][li][:, hs]
                    a[h] = a[h] + jnp.where(mask, _dot_nt(xh, xh), 0.0)
            att.append([x.astype(BF16) for x in a])
        for j, (u, d) in enumerate(ud):
            vv = i_ref[rows[u], :]
            o = [_dot(att[j][h], vv[:, hs]) for h, hs in enumerate(hsl)]
            outs[d][0][rows[u], 0:LANE] = (o[0] + o[1]).astype(BF16)
            outs[d][0][rows[u], LANE:2 * LANE] = (o[2] + o[3]).astype(BF16)
        return carry

    lax.fori_loop(0, nchunks // HG_UNROLL, group, 0)


def _hgrn_chunk(z, lb, tb=PAR_BLOCK):
    t = z.shape[0]
    nch = tb // HG_C
    ow = HG_HEADS * HG_DV
    zspec = lambda off: pl.BlockSpec((tb, HW), lambda i: (i, off // HW))
    wide = pl.BlockSpec((tb, HW), lambda i: (i, 0))
    narrow = pl.BlockSpec((tb, ow), lambda i: (i, 0))
    espec = pl.BlockSpec((nch * SUBLANE, HW), lambda i: (i, 0))
    ws, ns = jax.ShapeDtypeStruct((t, HW), BF16), jax.ShapeDtypeStruct((t, ow), BF16)
    es = jax.ShapeDtypeStruct((t // HG_C * SUBLANE, HW), F32)
    return pl.pallas_call(
        functools.partial(_hgrn_chunk_kernel, nchunks=nch), name="hgrn_chunk",
        grid=(t // tb,),
        in_specs=[zspec(Z_HQ), zspec(Z_HFF), zspec(Z_HFB), zspec(Z_HI),
                  pl.BlockSpec((2, 1, HW), lambda i: (0, 0, 0))],
        out_specs=[narrow, narrow, wide, wide, wide, wide, espec, espec],
        out_shape=[ns, ns, ws, ws, ws, ws, es, es],
        compiler_params=_cp(("parallel",)),
    )(z, z, z, z, lb.reshape(2, 1, HW))


def _hgrn_state_kernel(oif_ref, oib_ref, qdf_ref, qdb_ref, kef_ref, keb_ref, ef_ref, eb_ref, vf_ref, vb_ref,
                       of_ref, ob_ref, s_ref, *, nchunks):
    c = HG_C

    @pl.when(pl.program_id(1) == 0)
    def _():
        s_ref[...] = jnp.zeros_like(s_ref)

    dirs = ((oif_ref, qdf_ref, kef_ref, ef_ref, vf_ref, of_ref), (oib_ref, qdb_ref, keb_ref, eb_ref, vb_ref, ob_ref))
    state = [[s_ref[d, h] for h in range(HG_HEADS)] for d in range(2)]
    hsl = [slice(h * LANE, (h + 1) * LANE) for h in range(HG_HEADS)]
    chunk_of = lambda ci, d: ci if d == 0 else nchunks - 1 - ci
    rows_of = lambda ci, d: slice(chunk_of(ci, d) * c, (chunk_of(ci, d) + 1) * c)
    inc = [[[_dot_tn(dirs[d][4][rows_of(ci, d), hs], dirs[d][2][rows_of(ci, d), hs]) for hs in hsl]
            for d in range(2)] for ci in range(nchunks)]
    for ci in range(nchunks):
        inter = [[_dot_nt(dirs[d][1][rows_of(ci, d), hs], state[d][h].astype(BF16)) for h, hs in enumerate(hsl)]
                 for d in range(2)]
        for d, (oi_ref, qd_ref, ke_ref, e_ref, v_ref, o_ref) in enumerate(dirs):
            rows, cc = rows_of(ci, d), chunk_of(ci, d)
            for h, hs in enumerate(hsl):
                state[d][h] = state[d][h] * e_ref[cc * SUBLANE:cc * SUBLANE + 1, hs] + inc[ci][d][h]
            oi = oi_ref[rows, :].astype(F32)
            o_ref[rows, 0:LANE] = (oi[:, 0:LANE] + inter[d][0] + inter[d][1]).astype(BF16)
            o_ref[rows, LANE:2 * LANE] = (oi[:, LANE:2 * LANE] + inter[d][2] + inter[d][3]).astype(BF16)
    for d in range(2):
        for h in range(HG_HEADS):
            s_ref[d, h] = state[d][h]


def _hgrn_state(z, oi0, oi1, qd0, qd1, ke0, ke1, e0, e1, batch, seq, ctx_len, tb=SCAN_BLOCK):
    t = z.shape[0]
    nlb, ncb = seq // tb, ctx_len // tb
    nch = tb // HG_C
    ow = HG_HEADS * HG_DV
    blk = functools.partial(_scan_block, n_lat_blocks=nlb, n_ctx_blocks=ncb, batch=batch)
    fwd = lambda b, j: (blk(b, 0, j), 0)
    bwd = lambda b, j: (blk(b, 1, j), 0)
    both = lambda shape: [pl.BlockSpec(shape, fwd), pl.BlockSpec(shape, bwd)]
    vcol = Z_HI // HW
    os_ = jax.ShapeDtypeStruct((t, ow), BF16)
    return pl.pallas_call(
        functools.partial(_hgrn_state_kernel, nchunks=nch), name="hgrn_state",
        grid=(batch, nlb + ncb),
        in_specs=(both((tb, ow)) + both((tb, HW)) + both((tb, HW)) + both((nch * SUBLANE, HW))
                  + [pl.BlockSpec((tb, HW), lambda b, j: (blk(b, 0, j), vcol)),
                     pl.BlockSpec((tb, HW), lambda b, j: (blk(b, 1, j), vcol))]),
        out_specs=both((tb, ow)),
        out_shape=[os_, os_],
        scratch_shapes=[pltpu.VMEM((2, HG_HEADS, LANE, HG_DK), F32)],
        compiler_params=_cp(("parallel", "arbitrary")),
    )(oi0, oi1, qd0, qd1, ke0, ke1, e0, e1, z, z)


def _mla_prep_kernel(cq_ref, ckv_ref, kr_ref, krp_ref, cos_ref, sin_ref, gq_ref, gkv_ref,
                     wqa_ref, wqb_ref, wk_ref, wv_ref, q_ref, k_ref, v_ref):
    cos = cos_ref[...]
    sin = sin_ref[...]
    cqn = _rms(cq_ref[...].astype(F32), gq_ref[...]).astype(BF16)
    qa = _dot(cqn, wqa_ref[...])
    qb = _dot(cqn, wqb_ref[...])
    scale = (MLA_NOPE + MLA_ROPE) ** -0.5 * LOG2E
    ckvn = _rms(ckv_ref[...].astype(F32), gkv_ref[...]).astype(BF16)
    kn = _dot(ckvn, wk_ref[...])
    kr = kr_ref[...].astype(F32) * cos + krp_ref[...].astype(F32) * sin
    for h in range(MLA_HEADS):
        hs = slice(h * LANE, (h + 1) * LANE)
        q_ref[:, hs] = ((qa[:, hs] * cos + qb[:, hs] * sin) * scale).astype(BF16)
        k_ref[:, hs] = (kn[:, hs] + kr).astype(BF16)
    lane = lax.broadcasted_iota(jnp.int32, (1, MLA_HEADS * LANE), 1)
    ones_lane = functools.reduce(jnp.logical_or, [lane == h * LANE + _den_lane(h) for h in range(MLA_HEADS)])
    v_ref[...] = (_dot(ckvn, wv_ref[...]) + jnp.where(ones_lane, 1.0, 0.0)).astype(BF16)


def _mla_prep(z, cos_t, sin_t, gq, gkv, wqa, wqb, wk, wv, n_lat_rows, seq, tm=PAR_BLOCK):
    t = z.shape[0]
    lat_tiles, seq_tiles = n_lat_rows // tm, seq // tm
    hw = MLA_HEADS * LANE

    def tab(i):
        return (jnp.where(i < lat_tiles, i % seq_tiles, seq_tiles), 0)

    full = lambda a: pl.BlockSpec(a.shape, lambda i: (0, 0))
    return pl.pallas_call(
        _mla_prep_kernel, name="mla_prep",
        grid=(t // tm,),
        in_specs=[pl.BlockSpec((tm, MLA_Q_RANK), lambda i: (i, Z_CQ // MLA_Q_RANK)),
                  pl.BlockSpec((tm, LANE), lambda i: (i, Z_CKV // LANE)),
                  pl.BlockSpec((tm, LANE), lambda i: (i, Z_KR // LANE)),
                  pl.BlockSpec((tm, LANE), lambda i: (i, Z_KRP // LANE)),
                  pl.BlockSpec((tm, LANE), tab),
                  pl.BlockSpec((tm, LANE), tab),
                  full(gq), full(gkv), full(wqa), full(wqb), full(wk), full(wv)],
        out_specs=[pl.BlockSpec((tm, hw), lambda i: (i, 0))] * 3,
        out_shape=[jax.ShapeDtypeStruct((t, hw), BF16)] * 3,
        compiler_params=_cp(("parallel",)),
    )(z, z, z, z, cos_t, sin_t, gq, gkv, wqa, wqb, wk, wv)


def _attn_kernel(q_ref, kl_ref, vl_ref, kc_ref, vc_ref, o_ref, *, lat_tiles):
    i = pl.program_id(1)

    lane = lax.broadcasted_iota(jnp.int32, (1, LANE), 1)

    def scores(h, use_latent):
        hs = slice(h * LANE, (h + 1) * LANE)
        q = q_ref[:, hs]
        return (_dot_nt(q, kc_ref[:, hs]), _dot_nt(q, kl_ref[:, hs]) if use_latent else None)

    def weighted(h, s, use_latent):
        hs = slice(h * LANE, (h + 1) * LANE)
        sc, sl = s
        m = jnp.max(sc, axis=-1, keepdims=True)
        if use_latent:
            m = jnp.maximum(m, jnp.max(sl, axis=-1, keepdims=True))
        acc = _dot(jnp.exp2((sc - m).astype(BF16)), vc_ref[:, hs])
        if use_latent:
            acc = acc + _dot(jnp.exp2((sl - m).astype(BF16)), vl_ref[:, hs])
        dl = _den_lane(h)
        den = acc[:, dl:dl + 1]
        return jnp.where(lane == dl, 0.0, acc / den)

    def run(use_latent):
        halves = []
        s_next = scores(0, use_latent)
        for h in range(MLA_HEADS):
            s_cur = s_next
            if h + 1 < MLA_HEADS:
                s_next = scores(h + 1, use_latent)
            halves.append(weighted(h, s_cur, use_latent))
        o_ref[:, 0:LANE] = (halves[0] + halves[1]).astype(BF16)
        o_ref[:, LANE:2 * LANE] = (halves[2] + halves[3]).astype(BF16)

    @pl.when(i < lat_tiles)
    def _():
        run(True)

    @pl.when(i >= lat_tiles)
    def _():
        run(False)


def _attention(q, k, v, batch, seq, ctx_len, with_ctx, tq=256):
    t = q.shape[0]
    hw = MLA_HEADS * LANE
    lat_tiles = seq // tq
    ctx_q_tiles = ctx_len // tq if with_ctx else 0
    ctx_base_q = batch * lat_tiles
    ctx_base_k = batch * seq // ctx_len

    def qmap(b, i):
        return (jnp.where(i < lat_tiles, b * lat_tiles + i,
                          ctx_base_q + b * (ctx_len // tq) + (i - lat_tiles)), 0)

    return pl.pallas_call(
        functools.partial(_attn_kernel, lat_tiles=lat_tiles), name="mla_attn",
        grid=(batch, lat_tiles + ctx_q_tiles),
        in_specs=[pl.BlockSpec((tq, hw), qmap),
                  pl.BlockSpec((seq, hw), lambda b, i: (b, 0)),
                  pl.BlockSpec((seq, hw), lambda b, i: (b, 0)),
                  pl.BlockSpec((ctx_len, hw), lambda b, i: (ctx_base_k + b, 0)),
                  pl.BlockSpec((ctx_len, hw), lambda b, i: (ctx_base_k + b, 0))],
        out_specs=pl.BlockSpec((tq, MLA_HEADS * MLA_DV), qmap),
        out_shape=jax.ShapeDtypeStruct((t, MLA_HEADS * MLA_DV), BF16),
        compiler_params=_cp(("parallel", "arbitrary")),
    )(q, k, v, k, v)


def _out_proj_kernel(h_ref, mod_ref, og0_ref, og1_ref, gg_ref, ym_ref, oh0_ref, oh1_ref, hg_ref,
                     gn_ref, hn_ref, wo_ref, gpost_ref, o_ref):
    tm = h_ref.shape[0]
    og = og0_ref[...].astype(F32) + og1_ref[...].astype(F32)
    gate = gg_ref[...].astype(F32)
    acc = jnp.zeros((tm, D_MODEL), F32)
    for h in range(GDN_HEADS):
        hs = slice(h * LANE, (h + 1) * LANE)
        g = gate[:, hs]
        y = _rms(og[:, hs], gn_ref[...]) * (g * _sigmoid(g))
        acc = acc + _dot(y.astype(BF16), wo_ref[h * LANE:(h + 1) * LANE, :])
    base = GDN_HEADS * GDN_DV
    acc = acc + _dot(ym_ref[...], wo_ref[base:base + MLA_HEADS * MLA_DV, :])
    base += MLA_HEADS * MLA_DV
    oh = oh0_ref[...].astype(F32) + oh1_ref[...].astype(F32)
    hgate = hg_ref[...].astype(F32)
    lane = lax.broadcasted_iota(jnp.int32, (1, LANE), 1)
    lo = lane < HG_DV
    for p in range(HG_HEADS // 2):
        ps = slice(p * LANE, (p + 1) * LANE)
        x = oh[:, ps]
        sq = x * x
        ms_lo = jnp.sum(jnp.where(lo, sq, 0.0), axis=-1, keepdims=True) / HG_DV
        ms_hi = jnp.sum(jnp.where(lo, 0.0, sq), axis=-1, keepdims=True) / HG_DV
        ms = jnp.where(lo, ms_lo, ms_hi)
        g = hgate[:, ps]
        y = x * lax.rsqrt(ms + EPS) * hn_ref[...] * (g * _sigmoid(g))
        acc = acc + _dot(y.astype(BF16), wo_ref[base + p * LANE:base + (p + 1) * LANE, :])
    m = mod_ref[0]
    o_ref[...] = h_ref[...] + m[:, 2 * D_MODEL:3 * D_MODEL] * _rms(acc, gpost_ref[...])


def _out_proj(h, mods, og0, og1, z, ym, oh0, oh1, gdn_norm, hgrn_norm, wo, gpost, n_rows, n_lat_rows, seq, tm=512):
    midx = lambda i: (_mod_index(i, tm, n_lat_rows, seq), 0, 0)
    ow = HG_HEADS * HG_DV
    full = lambda a: pl.BlockSpec(a.shape, lambda i: (0, 0))
    row = lambda w: pl.BlockSpec((tm, w), lambda i: (i, 0))
    hn2 = jnp.concatenate([hgrn_norm, hgrn_norm]).reshape(1, LANE)
    gn = gdn_norm.reshape(1, LANE)
    return pl.pallas_call(
        _out_proj_kernel, name="out_proj",
        grid=(n_rows // tm,),
        in_specs=[row(D_MODEL),
                  pl.BlockSpec((1, 1, 6 * D_MODEL), midx),
                  row(GW), row(GW),
                  pl.BlockSpec((tm, GW), lambda i: (i, Z_GG // GW)),
                  row(MLA_HEADS * MLA_DV),
                  row(ow), row(ow),
                  pl.BlockSpec((tm, ow), lambda i: (i, Z_HGATE // ow)),
                  full(gn), full(hn2), full(wo),
                  pl.BlockSpec((1, D_MODEL), lambda i: (0, 0))],
        out_specs=row(D_MODEL),
        out_shape=jax.ShapeDtypeStruct((n_rows, D_MODEL), F32),
        compiler_params=_cp(("parallel",)),
    )(h, mods, og0, og1, z, ym, oh0, oh1, z, gn, hn2, wo, gpost.reshape(1, D_MODEL))


def _mlp_kernel(h_ref, mod_ref, gpre_ref, gpost_ref, w1_ref, w2_ref, o_ref, hm_ref, acc_ref):
    f = pl.program_id(1)

    @pl.when(f == 0)
    def _():
        m = mod_ref[0]
        y = _rms(h_ref[...], gpre_ref[...])
        hm_ref[...] = (y * (1.0 + m[:, 4 * D_MODEL:5 * D_MODEL]) + m[:, 3 * D_MODEL:4 * D_MODEL]).astype(BF16)
        acc_ref[...] = jnp.zeros_like(acc_ref)

    a = jnp.maximum(_dot(hm_ref[...], w1_ref[...]), 0.0)
    acc_ref[...] += _dot((a * a).astype(BF16), w2_ref[...])

    @pl.when(f == pl.num_programs(1) - 1)
    def _():
        m = mod_ref[0]
        o_ref[...] = h_ref[...] + m[:, 5 * D_MODEL:6 * D_MODEL] * _rms(acc_ref[...], gpost_ref[...])


def _mlp(h, mods, gpre, gpost, w1, w2, n_rows, n_lat_rows, seq, tm=1024, tf=1024):
    midx = lambda i, f: (_mod_index(i, tm, n_lat_rows, seq), 0, 0)
    return pl.pallas_call(
        _mlp_kernel, name="mlp",
        grid=(n_rows // tm, D_FF // tf),
        in_specs=[pl.BlockSpec((tm, D_MODEL), lambda i, f: (i, 0)),
                  pl.BlockSpec((1, 1, 6 * D_MODEL), midx),
                  pl.BlockSpec((1, D_MODEL), lambda i, f: (0, 0)),
                  pl.BlockSpec((1, D_MODEL), lambda i, f: (0, 0)),
                  pl.BlockSpec((D_MODEL, tf), lambda i, f: (0, f)),
                  pl.BlockSpec((tf, D_MODEL), lambda i, f: (f, 0))],
        out_specs=pl.BlockSpec((tm, D_MODEL), lambda i, f: (i, 0)),
        out_shape=jax.ShapeDtypeStruct((n_rows, D_MODEL), F32),
        scratch_shapes=[pltpu.VMEM((tm, D_MODEL), BF16), pltpu.VMEM((tm, D_MODEL), F32)],
        compiler_params=_cp(("parallel", "arbitrary")),
    )(h, mods, gpre.reshape(1, D_MODEL), gpost.reshape(1, D_MODEL), w1, w2)


def _arrange_w_in(w):
    bounds, tot = [], 0
    for sz in IN_SIZES[:-1]:
        tot += sz
        bounds.append(tot)
    (gq, gk, gv, gg, gb, ga, cq, ckv, kr, hq, hff, hfb, hi, hgate) = jnp.split(w, bounds, axis=1)
    zc = lambda n: jnp.zeros((w.shape[0], n), w.dtype)
    ba = jnp.concatenate([gb[:, 0:4], ga[:, 0:4], gb[:, 4:8], ga[:, 4:8], zc(LANE - 16)], axis=1)
    half = MLA_ROPE // 2
    krb = jnp.concatenate([zc(MLA_NOPE), kr, zc(LANE - MLA_NOPE - MLA_ROPE)], axis=1)
    krp = jnp.concatenate([zc(MLA_NOPE), -kr[:, half:], kr[:, :half], zc(LANE - MLA_NOPE - MLA_ROPE)], axis=1)
    his = []
    for h in range(HG_HEADS):
        blk = hi[:, h * HG_DV:(h + 1) * HG_DV]
        his += [blk, zc(HG_DV)] if h % 2 == 0 else [zc(HG_DV), blk]
    out = jnp.concatenate([gq, gk, gv, gg, hq, hff, hfb] + his + [hgate, cq, ba, ckv, krb, krp], axis=1)
    return out.astype(BF16)


def _arrange_mla(w_uq, w_ukv):
    half = MLA_ROPE // 2
    dq = MLA_NOPE + MLA_ROPE
    zq = lambda n: jnp.zeros((w_uq.shape[0], n), w_uq.dtype)
    zk = lambda n: jnp.zeros((w_ukv.shape[0], n), w_ukv.dtype)
    qa, qb, wk, wv = [], [], [], []
    for h in range(MLA_HEADS):
        nope = w_uq[:, h * dq:h * dq + MLA_NOPE]
        rope = w_uq[:, h * dq + MLA_NOPE:(h + 1) * dq]
        qa += [nope, rope, zq(LANE - dq)]
        qb += [zq(MLA_NOPE), -rope[:, half:], rope[:, :half], zq(LANE - dq)]
        kv = w_ukv[:, h * (MLA_NOPE + MLA_DV):(h + 1) * (MLA_NOPE + MLA_DV)]
        wk += [kv[:, :MLA_NOPE], zk(LANE - MLA_NOPE)]
        wv += [kv[:, MLA_NOPE:], zk(MLA_DV)] if h % 2 == 0 else [zk(MLA_DV), kv[:, MLA_NOPE:]]
    cat = lambda xs: jnp.concatenate(xs, axis=1).astype(BF16)
    return cat(qa), cat(qb), cat(wk), cat(wv)


def _rope_tables(seq, tile):
    per_axis = MLA_ROPE // 2
    inv = ROPE_BASE ** (-jnp.arange(0, per_axis, 2, dtype=F32) / per_axis)
    rows = seq // GRID_W
    row = jnp.repeat(jnp.arange(rows, dtype=F32), GRID_W)
    col = jnp.tile(jnp.arange(GRID_W, dtype=F32), rows)
    ang = jnp.concatenate([row[:, None] * inv, col[:, None] * inv], axis=-1)
    cos, sin = jnp.cos(ang), jnp.sin(ang)
    one = jnp.ones((seq, MLA_NOPE), F32)
    pad1 = jnp.ones((seq, LANE - MLA_NOPE - MLA_ROPE), F32)
    cos_t = jnp.concatenate([one, cos, cos, pad1], axis=1)
    sin_t = jnp.concatenate([0 * one, sin, sin, 0 * pad1], axis=1)
    cos_t = jnp.concatenate([cos_t, jnp.ones((tile, LANE), F32)], axis=0)
    sin_t = jnp.concatenate([sin_t, jnp.zeros((tile, LANE), F32)], axis=0)
    return cos_t, sin_t


def kernel(x, c, ctx, c_ctx, w_ada, b_ada, norm_mix_pre, norm_mix_post, norm_mlp_pre, norm_mlp_post,
           w_in, w_out, gdn_conv, gdn_a_log, gdn_dt_bias, gdn_norm, mla_q_norm, mla_kv_norm,
           mla_w_uq, mla_w_ukv, hgrn_lb_logits, hgrn_norm, w_mlp1, w_mlp2):
    batch, seq, _ = x.shape
    ctx_len = ctx.shape[1]
    depth = w_in.shape[0]
    n_lat = batch * seq
    n_all = n_lat + batch * ctx_len
    assert seq % 1024 == 0 and ctx_len % SCAN_BLOCK == 0 and (batch * ctx_len) % 1024 == 0

    cos_t, sin_t = _rope_tables(seq, PAR_BLOCK)
    lb_cum = jnp.cumsum(jax.nn.softmax(hgrn_lb_logits.astype(F32), axis=0), axis=0)
    lower_bounds = lb_cum - lb_cum[0]
    cvec = jnp.concatenate([c_ctx[None, :], c, jnp.zeros((16 - 1 - batch, D_MODEL), F32)], axis=0)

    h = jnp.concatenate([x.reshape(n_lat, D_MODEL), ctx.reshape(batch * ctx_len, D_MODEL)], axis=0)
    for l in range(depth):
        last = l == depth - 1
        n_rows = n_lat if last else n_all
        mods = _ada(cvec, w_ada[l], b_ada[l])[:1 + batch].reshape(1 + batch, 1, 6 * D_MODEL)
        z = _in_proj(h, mods, norm_mix_pre[l], _arrange_w_in(w_in[l]), n_lat, seq)
        qkv = _gdn_prep(z, gdn_conv[l], n_lat, seq, ctx_len)
        gx0, gx1, ge0, ge1 = _gdn_chunk(qkv, z, gdn_a_log[l], gdn_dt_bias[l])
        og0, og1 = _gdn_state(gx0, gx1, ge0, ge1, batch, seq, ctx_len)
        wqa, wqb, wk, wv = _arrange_mla(mla_w_uq[l], mla_w_ukv[l])
        q, k, v = _mla_prep(z, cos_t, sin_t, mla_q_norm[l].reshape(1, -1), mla_kv_norm[l].reshape(1, -1),
                            wqa, wqb, wk, wv, n_lat, seq)
        ym = _attention(q, k, v, batch, seq, ctx_len, with_ctx=not last)
        oi0, oi1, qd0, qd1, ke0, ke1, he0, he1 = _hgrn_chunk(z, lower_bounds[l])
        oh0, oh1 = _hgrn_state(z, oi0, oi1, qd0, qd1, ke0, ke1, he0, he1, batch, seq, ctx_len)
        h = _out_proj(h, mods, og0, og1, z, ym, oh0, oh1, gdn_norm[l], hgrn_norm[l], w_out[l].astype(BF16),
                      norm_mix_post[l], n_rows, n_lat, seq)
        h = _mlp(h, mods, norm_mlp_pre[l], norm_mlp_post[l], w_mlp1[l].astype(BF16), w_mlp2[l].astype(BF16),
                 n_rows, n_lat, seq)
    return h[:n_lat].reshape(batch, seq, D_MODEL)
```
